```python
import math
import jax, jax.numpy as jnp
from jax import lax
import numpy as np

D_MODEL = 1024
BATCH = 8
SEQ = 4096
DEPTH = 2

CHUNK = 64
EPS = 1e-6
SSD_HEADS = 16
SSD_HEAD_DIM = 64
SSD_INNER = SSD_HEADS * SSD_HEAD_DIM
SSD_GROUPS = 2
HEADS_PER_GROUP = SSD_HEADS // SSD_GROUPS
SSD_STATE = 128
SSD_CONV = 4
SSD_CONV_DIM = SSD_INNER + 2 * SSD_GROUPS * SSD_STATE
ATT_HEADS = 8
ATT_QK_DIM = 64
ATT_V_DIM = 2 * ATT_QK_DIM
ATT_QK_WIDTH = ATT_HEADS * 2 * ATT_QK_DIM
ATT_V_WIDTH = ATT_HEADS * ATT_V_DIM
ROPE_THETA = 500000.0
ROPE_DIM = ATT_QK_DIM // 4
Q_BLOCK = 128
N_BRANCHES = 2
IN_COLS = SSD_INNER + SSD_CONV_DIM + SSD_HEADS + 2 * ATT_QK_WIDTH + ATT_V_WIDTH + N_BRANCHES * D_MODEL
D_FF = ((8 * D_MODEL // 3 + 127) // 128) * 128
N_EXPERTS = 8
TOP_K = 2
D_FF_EXPERT = D_FF
N_DENSE = (DEPTH + 1) // 2
N_MOE = DEPTH // 2

kernel_name = "hybrid_ssd_diffattn_moe_trunk"


def _rms(x):
    xf = x.astype(jnp.float32)
    return (xf * lax.rsqrt(jnp.mean(xf * xf, axis=-1, keepdims=True) + EPS)).astype(x.dtype)


def rmsnorm(x, w):
    return _rms(x) * w


def rope_tables(seq_len):
    inv = ROPE_THETA ** (-jnp.arange(0, ROPE_DIM, 2, dtype=jnp.float32) / ROPE_DIM)
    ang = jnp.arange(seq_len, dtype=jnp.float32)[:, None] * inv[None, :]
    return jnp.cos(ang), jnp.sin(ang)


def apply_partial_rope(t, cos, sin):
    rot, rest = t[..., :ROPE_DIM], t[..., ROPE_DIM:]
    r1, r2 = rot[..., :ROPE_DIM // 2], rot[..., ROPE_DIM // 2:]
    c = cos[None, :, None, None, :].astype(t.dtype)
    s = sin[None, :, None, None, :].astype(t.dtype)
    return jnp.concatenate([r1 * c - r2 * s, r2 * c + r1 * s, rest], axis=-1)


def causal_dwconv(u, w, b):
    out = lax.conv_general_dilated(
        u, w[:, None, :].astype(u.dtype), window_strides=(1,), padding=[(SSD_CONV - 1, 0)],
        dimension_numbers=("NWC", "WIO", "NWC"), feature_group_count=u.shape[-1])
    return out + b


def ssd_mixer(xbc, dt_raw, z, conv_w, conv_b, dt_bias, a_log, d_skip, norm_w):
    bsz, s_len, _ = xbc.shape
    nc = s_len // CHUNK
    xbc = jax.nn.silu(causal_dwconv(xbc, conv_w, conv_b))
    gn = SSD_GROUPS * SSD_STATE
    xs, bm, cm = jnp.split(xbc, [SSD_INNER, SSD_INNER + gn], axis=-1)
    xs = xs.reshape(bsz, nc, CHUNK, SSD_GROUPS, HEADS_PER_GROUP, SSD_HEAD_DIM)
    bm = bm.reshape(bsz, nc, CHUNK, SSD_GROUPS, SSD_STATE)
    cm = cm.reshape(bsz, nc, CHUNK, SSD_GROUPS, SSD_STATE)
    dt = jax.nn.softplus(dt_raw.astype(jnp.float32) + dt_bias.astype(jnp.float32))
    a = -jnp.exp(a_log.astype(jnp.float32))
    dt_c = dt.reshape(bsz, nc, CHUNK, SSD_GROUPS, HEADS_PER_GROUP)
    adt = jnp.moveaxis((dt * a).reshape(bsz, nc, CHUNK, SSD_GROUPS, HEADS_PER_GROUP), 2, -1)
    a_cs = jnp.cumsum(adt, axis=-1)
    xdt = xs * dt_c[..., None].astype(xs.dtype)
    seg = a_cs[..., :, None] - a_cs[..., None, :]
    causal = jnp.tril(jnp.ones((CHUNK, CHUNK), dtype=bool))
    decay_in = jnp.exp(jnp.where(causal, seg, -jnp.inf))
    cb = jnp.einsum('bclgn,bcsgn->bcgls', cm, bm)
    y_diag = jnp.einsum('bcgels,bcsgep->bclgep', cb[:, :, :, None] * decay_in, xdt)
    decay_to_end = jnp.exp(a_cs[..., -1:] - a_cs)
    states = jnp.einsum('bclgn,bcgel,bclgep->bcgepn', bm, decay_to_end, xdt)
    chunk_decay = jnp.exp(a_cs[..., -1])

    def step(h, inp):
        s_c, d_c = inp
        h_new = (d_c[..., None, None] * h + s_c).astype(h.dtype)
        return h_new, h

    h0 = jnp.zeros_like(states[:, 0])
    _, prev = lax.scan(step, h0, (jnp.moveaxis(states, 1, 0), jnp.moveaxis(chunk_decay, 1, 0)))
    prev = jnp.moveaxis(prev, 0, 1)
    y_off = jnp.einsum('bclgn,bcgepn,bcgel->bclgep', cm, prev, jnp.exp(a_cs))
    y = y_diag + y_off + xs * d_skip.reshape(SSD_GROUPS, HEADS_PER_GROUP)[:, :, None]
    y = y.reshape(bsz, s_len, SSD_INNER).astype(z.dtype)
    yz = (y * jax.nn.silu(z)).reshape(bsz, s_len, SSD_GROUPS, SSD_INNER // SSD_GROUPS)
    return _rms(yz).reshape(bsz, s_len, SSD_INNER) * norm_w


def diff_attention(q, k, v, qn_w, kn_w, lq1, lk1, lq2, lk2, subln_w, lambda_init, cos, sin):
    bsz, s_len, _ = q.shape
    q = q.reshape(bsz, s_len, ATT_HEADS, 2, ATT_QK_DIM)
    k = k.reshape(bsz, s_len, ATT_HEADS, 2, ATT_QK_DIM)
    v = v.reshape(bsz, s_len, ATT_HEADS, ATT_V_DIM)
    q = apply_partial_rope(rmsnorm(q, qn_w), cos, sin) * (ATT_QK_DIM ** -0.5)
    k = apply_partial_rope(rmsnorm(k, kn_w), cos, sin)
    lam = (jnp.exp(jnp.sum(lq1.astype(jnp.float32) * lk1.astype(jnp.float32)))
           - jnp.exp(jnp.sum(lq2.astype(jnp.float32) * lk2.astype(jnp.float32))) + lambda_init)
    nqb = s_len // Q_BLOCK
    qb = jnp.moveaxis(q.reshape(bsz, nqb, Q_BLOCK, ATT_HEADS, 2, ATT_QK_DIM), 1, 0)
    key_chunk = jnp.arange(s_len) // CHUNK

    def block(args):
        q_blk, i = args
        q_chunk = (i * Q_BLOCK + jnp.arange(Q_BLOCK)) // CHUNK
        allowed = key_chunk[None, :] <= q_chunk[:, None]
        sc = jnp.einsum('bqhjd,bkhjd->bhjqk', q_blk, k).astype(jnp.float32)
        p = jax.nn.softmax(jnp.where(allowed, sc, -jnp.inf), axis=-1)
        a_map = p[:, :, 0] - lam * p[:, :, 1]
        return jnp.einsum('bhqk,bkhe->bqhe', a_map.astype(v.dtype), v)

    o = lax.map(block, (qb, jnp.arange(nqb)))
    o = jnp.moveaxis(o, 0, 1).reshape(bsz, s_len, ATT_HEADS, ATT_V_DIM)
    o = rmsnorm(o, subln_w) * (1.0 - lambda_init)
    return o.reshape(bsz, s_len, ATT_V_WIDTH)


def swiglu(t, wg, wu, wd):
    return (jax.nn.silu(t @ wg) * (t @ wu)) @ wd


def moe_swiglu(h, router_w, wg, wu, wd):
    bsz, s_len, d = h.shape
    t = h.reshape(-1, d)
    logits = (t @ router_w).astype(jnp.float32)
    top_v, top_i = lax.top_k(logits, TOP_K)
    top_w = jax.nn.softmax(top_v, axis=-1)
    combine = jnp.sum(jax.nn.one_hot(top_i, N_EXPERTS, dtype=jnp.float32) * top_w[..., None], axis=1)
    out = jnp.zeros_like(t)
    for e in range(N_EXPERTS):
        out = out + combine[:, e:e + 1].astype(t.dtype) * swiglu(t, wg[e], wu[e], wd[e])
    return out.reshape(bsz, s_len, d)


def setup_inputs(seed: int = 0) -> dict:
    key = jax.random.key(seed)
    ks = jax.random.split(key, 32)
    f32 = jnp.float32
    nrm = lambda k, shape, scale: jax.random.normal(k, shape, f32) * scale
    gain = lambda k, shape: 1.0 + 0.01 * jax.random.normal(k, shape, f32)
    dt0 = jnp.exp(jax.random.uniform(ks[5], (DEPTH, SSD_HEADS), f32, math.log(1e-3), math.log(1e-1)))
    return {
        "x": nrm(ks[0], (BATCH, SEQ, D_MODEL), 1.0),
        "norm_mix_w": gain(ks[1], (DEPTH, D_MODEL)),
        "w_in": nrm(ks[2], (DEPTH, D_MODEL, IN_COLS), D_MODEL ** -0.5),
        "conv_w": nrm(ks[3], (DEPTH, SSD_CONV, SSD_CONV_DIM), SSD_CONV ** -0.5),
        "conv_b": nrm(ks[4], (DEPTH, SSD_CONV_DIM), 0.01),
        "dt_bias": dt0 + jnp.log(-jnp.expm1(-dt0)),
        "a_log": jnp.log(jax.random.uniform(ks[6], (DEPTH, SSD_HEADS), f32, 1.0, 16.0)),
        "d_skip": gain(ks[7], (DEPTH, SSD_HEADS)),
        "ssd_norm_w": gain(ks[8], (DEPTH, SSD_INNER)),
        "q_norm_w": gain(ks[9], (DEPTH, ATT_QK_DIM)),
        "k_norm_w": gain(ks[10], (DEPTH, ATT_QK_DIM)),
        "lambda_q1": nrm(ks[11], (DEPTH, ATT_QK_DIM), 0.1),
        "lambda_k1": nrm(ks[12], (DEPTH, ATT_QK_DIM), 0.1),
        "lambda_q2": nrm(ks[13], (DEPTH, ATT_QK_DIM), 0.1),
        "lambda_k2": nrm(ks[14], (DEPTH, ATT_QK_DIM), 0.1),
        "subln_w": gain(ks[15], (DEPTH, ATT_V_DIM)),
        "gate_b": nrm(ks[16], (DEPTH, N_BRANCHES, D_MODEL), 0.01),
        "w_br_ssd": nrm(ks[17], (DEPTH, SSD_INNER, D_MODEL), SSD_INNER ** -0.5),
        "w_br_att": nrm(ks[18], (DEPTH, ATT_V_WIDTH, D_MODEL), ATT_V_WIDTH ** -0.5),
        "w_out": nrm(ks[19], (DEPTH, D_MODEL, D_MODEL), D_MODEL ** -0.5),
        "norm_ffn_w": gain(ks[20], (DEPTH, D_MODEL)),
        "ffn_w_gate": nrm(ks[21], (N_DENSE, D_MODEL, D_FF), D_MODEL ** -0.5),
        "ffn_w_up": nrm(ks[22], (N_DENSE, D_MODEL, D_FF), D_MODEL ** -0.5),
        "ffn_w_down": nrm(ks[23], (N_DENSE, D_FF, D_MODEL), D_FF ** -0.5),
        "router_w": nrm(ks[24], (N_MOE, D_MODEL, N_EXPERTS), D_MODEL ** -0.5),
        "moe_w_gate": nrm(ks[25], (N_MOE, N_EXPERTS, D_MODEL, D_FF_EXPERT), D_MODEL ** -0.5),
        "moe_w_up": nrm(ks[26], (N_MOE, N_EXPERTS, D_MODEL, D_FF_EXPERT), D_MODEL ** -0.5),
        "moe_w_down": nrm(ks[27], (N_MOE, N_EXPERTS, D_FF_EXPERT, D_MODEL), D_FF_EXPERT ** -0.5),
    }


def reference(x, norm_mix_w, w_in, conv_w, conv_b, dt_bias, a_log, d_skip, ssd_norm_w,
              q_norm_w, k_norm_w, lambda_q1, lambda_k1, lambda_q2, lambda_k2, subln_w,
              gate_b, w_br_ssd, w_br_att, w_out, norm_ffn_w, ffn_w_gate, ffn_w_up, ffn_w_down,
              router_w, moe_w_gate, moe_w_up, moe_w_down):
    s_len = x.shape[1]
    cos, sin = rope_tables(s_len)
    o1 = SSD_INNER
    o2 = o1 + SSD_CONV_DIM
    o3 = o2 + SSD_HEADS
    o4 = o3 + ATT_QK_WIDTH
    o5 = o4 + ATT_QK_WIDTH
    o6 = o5 + ATT_V_WIDTH
    o7 = o6 + D_MODEL
    for i in range(DEPTH):
        lambda_init = 0.8 - 0.6 * math.exp(-0.3 * i)
        xn = rmsnorm(x, norm_mix_w[i])
        proj = xn @ w_in[i]
        z, xbc, dt_raw, q, k, v, g_s, g_a = jnp.split(proj, [o1, o2, o3, o4, o5, o6, o7], axis=-1)
        y_ssd = ssd_mixer(xbc, dt_raw, z, conv_w[i], conv_b[i], dt_bias[i], a_log[i],
                          d_skip[i], ssd_norm_w[i])
        y_att = diff_attention(q, k, v, q_norm_w[i], k_norm_w[i], lambda_q1[i], lambda_k1[i],
                               lambda_q2[i], lambda_k2[i], subln_w[i], lambda_init, cos, sin)
        merged = (jax.nn.sigmoid(g_s + gate_b[i, 0]) * (y_ssd @ w_br_ssd[i])
                  + jax.nn.sigmoid(g_a + gate_b[i, 1]) * (y_att @ w_br_att[i]))
        x = x + merged @ w_out[i]
        hn = rmsnorm(x, norm_ffn_w[i])
        j = i // 2
        if i % 2 == 0:
            x = x + swiglu(hn, ffn_w_gate[j], ffn_w_up[j], ffn_w_down[j])
        else:
            x = x + moe_swiglu(hn, router_w[j], moe_w_gate[j], moe_w_up[j], moe_w_down[j])
    return x
```

```python
import functools
import math

import numpy as np
import jax
import jax.numpy as jnp
from jax import lax
from jax.experimental import pallas as pl
from jax.experimental.pallas import tpu as pltpu

F32 = jnp.float32
BF16 = jnp.bfloat16

D_MODEL = 1024
CHUNK = 64
EPS = 1e-6
SSD_HEADS = 16
SSD_HEAD_DIM = 64
SSD_INNER = SSD_HEADS * SSD_HEAD_DIM
SSD_GROUPS = 2
SSD_STATE = 128
SSD_CONV = 4
SSD_CONV_DIM = SSD_INNER + 2 * SSD_GROUPS * SSD_STATE
ATT_HEADS = 8
ATT_QK_DIM = 64
ATT_V_DIM = 128
ATT_WIDTH = ATT_HEADS * ATT_V_DIM
ROPE_THETA = 500000.0
ROPE_DIM = ATT_QK_DIM // 4
D_FF = 2816
N_EXPERTS = 8
LANES = 128
GROUP_W = SSD_INNER // SSD_GROUPS
MAIN_COLS = 6 * D_MODEL + SSD_CONV_DIM
VMEM_LIMIT = 56 * 1024 * 1024


def _cparams(sem):
    return pltpu.CompilerParams(dimension_semantics=sem, vmem_limit_bytes=VMEM_LIMIT)


def _split3(a):
    h1 = a.astype(BF16)
    r1 = a - h1.astype(F32)
    h2 = r1.astype(BF16)
    r2 = r1 - h2.astype(F32)
    return h1, h2, r2.astype(BF16)


def _dot(a, b):
    return jnp.dot(a, b, preferred_element_type=F32)


def _sigmoid(x):
    return 1.0 / (1.0 + jnp.exp(-x))


def _silu(x):
    return x * _sigmoid(x)


def _inproj_kernel(x_ref, nw_ref, w_ref, wdt_ref, o_ref, dt_ref, xn_ref):
    @pl.when(pl.program_id(1) == 0)
    def _():
        x = x_ref[...]
        ms = jnp.mean(x * x, axis=-1, keepdims=True)
        xn = x * lax.rsqrt(ms + EPS) * nw_ref[...]
        x1, x2, _ = _split3(xn)
        xn_ref[...] = x1
        dt_ref[...] = _dot(x1, wdt_ref[0]) + _dot(x1, wdt_ref[1]) + _dot(x2, wdt_ref[0])

    o_ref[...] = _dot(xn_ref[...], w_ref[...]).astype(o_ref.dtype)


def _in_proj(x2d, norm_w, w_main, w_dt, tm, tn):
    t, d = x2d.shape
    n = w_main.shape[1]
    return pl.pallas_call(
        _inproj_kernel,
        grid=(t // tm, n // tn),
        in_specs=[
            pl.BlockSpec((tm, d), lambda i, j: (i, 0)),
            pl.BlockSpec((1, d), lambda i, j: (0, 0)),
            pl.BlockSpec((d, tn), lambda i, j: (0, j)),
            pl.BlockSpec((2, d, LANES), lambda i, j: (0, 0, 0)),
        ],
        out_specs=[
            pl.BlockSpec((tm, tn), lambda i, j: (i, j)),
            pl.BlockSpec((tm, LANES), lambda i, j: (i, 0)),
        ],
        out_shape=[jax.ShapeDtypeStruct((t, n), BF16), jax.ShapeDtypeStruct((t, LANES), F32)],
        scratch_shapes=[pltpu.VMEM((tm, d), BF16)],
        compiler_params=_cparams(("parallel", "arbitrary")),
        name="in_proj",
    )(x2d, norm_w, w_main, w_dt)


def _ssd_kernel(xbc_ref, z_ref, dt_ref, convw_ref, convb_ref, dtb_ref, aexp_ref, dexp_ref, nw_ref,
                e_ref, tril_ref, dmask_ref, trilmask_ref, bdmask_ref,
                o_ref, xpad_ref, u_ref, dte_ref, state_ref, *, lb):
    @pl.when(pl.program_id(1) == 0)
    def _():
        xpad_ref[0:8, :] = jnp.zeros((8, SSD_CONV_DIM), F32)
        state_ref[...] = jnp.zeros_like(state_ref)

    xpad_ref[8:8 + lb, :] = xbc_ref[...].astype(F32)
    acc = convb_ref[...] + convw_ref[0:1, :] * xpad_ref[pl.ds(8 - (SSD_CONV - 1), lb), :]
    for k in range(1, SSD_CONV):
        acc = acc + convw_ref[k:k + 1, :] * xpad_ref[pl.ds(8 - (SSD_CONV - 1) + k, lb), :]
    u_ref[...] = _silu(acc)
    xpad_ref[0:8, :] = xpad_ref[lb:lb + 8, :]

    t = dt_ref[...] + dtb_ref[...]
    dt = jnp.maximum(t, 0.0) + jnp.log(1.0 + jnp.exp(-jnp.abs(t)))
    d1, d2, d3 = _split3(dt)
    e = e_ref[...]
    dte_ref[...] = _dot(d1, e) + _dot(d2, e) + _dot(d3, e)

    tril = tril_ref[...]
    dmask = dmask_ref[...]
    trilmask = trilmask_ref[...] > 0.5
    bdmask = bdmask_ref[...]
    aexp = aexp_ref[...]
    dexp = dexp_ref[...]
    nw = nw_ref[...]

    def chunk(j, carry):
        r0 = pl.multiple_of(j * CHUNK, CHUNK)
        rows = pl.ds(r0, CHUNK)
        xs = u_ref[rows, 0:SSD_INNER]
        bm = u_ref[rows, SSD_INNER:SSD_INNER + SSD_GROUPS * SSD_STATE]
        cm = u_ref[rows, SSD_INNER + SSD_GROUPS * SSD_STATE:SSD_CONV_DIM]
        dte = dte_ref[rows, :]
        a1, a2, a3 = _split3(dte * aexp)
        cs = _dot(tril, a1) + _dot(tril, a2) + _dot(tril, a3)
        rowv = jnp.sum(cs * dmask, axis=0, keepdims=True)
        decay = jnp.exp(jnp.where(trilmask, cs - rowv, -1e30))
        a_end = cs[CHUNK - 1:CHUNK, :]
        to_end = jnp.exp(a_end - cs)
        xdt = xs * dte
        xdt_b = xdt.astype(BF16)
        xw_b = (xdt * to_end).astype(BF16)
        bm_b = bm.astype(BF16)
        cm_b = cm.astype(BF16)

        cb_parts = []
        yoff_parts = []
        for g in range(SSD_GROUPS):
            bg = bm_b[:, g * SSD_STATE:(g + 1) * SSD_STATE]
            cg = cm_b[:, g * SSD_STATE:(g + 1) * SSD_STATE]
            cb = lax.dot_general(cg, bg, (((1,), (1,)), ((), ())), preferred_element_type=F32)
            cb2 = jnp.concatenate([cb, cb], axis=1)
            cb_parts += [cb2] * (GROUP_W // LANES)
            st = state_ref[g]
            yoff_parts.append(_dot(cg, st.astype(BF16)))
            bg_t = jnp.transpose(bm[:, g * SSD_STATE:(g + 1) * SSD_STATE]).astype(BF16)
            upd = _dot(bg_t, xw_b[:, g * GROUP_W:(g + 1) * GROUP_W])
            state_ref[g] = jnp.exp(a_end[:, g * GROUP_W:(g + 1) * GROUP_W]) * st + upd
        m_b = (jnp.concatenate(cb_parts, axis=1) * decay).astype(BF16)
        y_off = jnp.concatenate(yoff_parts, axis=1) * jnp.exp(cs)

        yd_parts = []
        for q in range(SSD_INNER // 256):
            sl = slice(q * 256, (q + 1) * 256)
            rhs = jnp.concatenate([xdt_b[:, sl]] * 4, axis=0) * bdmask
            yd_parts.append(_dot(m_b[:, sl], rhs))
        y = jnp.concatenate(yd_parts, axis=1) + y_off + xs * dexp

        yz = y * _silu(z_ref[rows, :].astype(F32))
        outs = []
        for g in range(SSD_GROUPS):
            yg = yz[:, g * GROUP_W:(g + 1) * GROUP_W]
            ms = jnp.mean(yg * yg, axis=-1, keepdims=True)
            outs.append(yg * lax.rsqrt(ms + EPS))
        o_ref[rows, :] = (jnp.concatenate(outs, axis=1) * nw).astype(o_ref.dtype)
        return carry

    lax.fori_loop(0, lb // CHUNK, chunk, 0)


def _ssd(proj, dt_raw, conv_w, conv_b, dtb, aexp, dexp, norm_w, consts, bsz, s_len, lb):
    t = proj.shape[0]
    nblk = s_len // lb
    xbc_blk0 = (6 * D_MODEL) // SSD_CONV_DIM
    full = lambda shape: pl.BlockSpec(shape, lambda b, c: (0,) * len(shape))
    e_mat, tril, dmask, trilmask, bdmask = consts
    return pl.pallas_call(
        functools.partial(_ssd_kernel, lb=lb),
        grid=(bsz, nblk),
        in_specs=[
            pl.BlockSpec((lb, SSD_CONV_DIM), lambda b, c: (b * nblk + c, xbc_blk0)),
            pl.BlockSpec((lb, SSD_INNER), lambda b, c: (b * nblk + c, 0)),
            pl.BlockSpec((lb, LANES), lambda b, c: (b * nblk + c, 0)),
            full((SSD_CONV, SSD_CONV_DIM)),
            full((1, SSD_CONV_DIM)),
            full((1, LANES)),
            full((1, SSD_INNER)),
            full((1, SSD_INNER)),
            full((1, SSD_INNER)),
            full(e_mat.shape), full(tril.shape), full(dmask.shape), full(trilmask.shape), full(bdmask.shape),
        ],
        out_specs=pl.BlockSpec((lb, SSD_INNER), lambda b, c: (b * nblk + c, 0)),
        out_shape=jax.ShapeDtypeStruct((t, SSD_INNER), BF16),
        scratch_shapes=[
            pltpu.VMEM((lb + 8, SSD_CONV_DIM), F32),
            pltpu.VMEM((lb, SSD_CONV_DIM), F32),
            pltpu.VMEM((lb, SSD_INNER), F32),
            pltpu.VMEM((SSD_GROUPS, SSD_STATE, GROUP_W), F32),
        ],
        compiler_params=_cparams(("parallel", "arbitrary")),
        name="ssd",
    )(proj, proj, dt_raw, conv_w, conv_b, dtb, aexp, dexp, norm_w, e_mat, tril, dmask, trilmask, bdmask)


def _qkprep_kernel(q_ref, k_ref, qw_ref, kw_ref, cos_ref, sa_ref, sb_ref, g_ref, qo_ref, kto_ref):
    rep = ATT_WIDTH // LANES
    cos = jnp.concatenate([cos_ref[...]] * rep, axis=1)
    sa = jnp.concatenate([sa_ref[...]] * rep, axis=1)
    sb = jnp.concatenate([sb_ref[...]] * rep, axis=1)
    gmat = g_ref[...]

    def prep(x, w):
        sq = (x * x).astype(BF16)
        ss = jnp.concatenate([_dot(sq[:, c * 256:(c + 1) * 256], gmat) for c in range(ATT_WIDTH // 256)], axis=1)
        xn = x * lax.rsqrt(ss * (1.0 / ATT_QK_DIM) + EPS) * w
        half = ROPE_DIM // 2
        slabs = [xn[:, c * LANES:(c + 1) * LANES] for c in range(rep)]
        up = jnp.concatenate([pltpu.roll(s, LANES - half, 1) for s in slabs], axis=1)
        dn = jnp.concatenate([pltpu.roll(s, half, 1) for s in slabs], axis=1)
        return xn * cos + up * sa + dn * sb

    q = prep(q_ref[...].astype(F32), qw_ref[...]) * (ATT_QK_DIM ** -0.5)
    qo_ref[...] = q.astype(qo_ref.dtype)
    k = prep(k_ref[...].astype(F32), kw_ref[...])
    kto_ref[0] = jnp.transpose(k).astype(kto_ref.dtype)


def _qk_prep(proj, qw, kw, cos_t, sa_t, sb_t, gmat, bsz, s_len, tm):
    t = proj.shape[0]
    nblk = s_len // tm
    full = lambda shape: pl.BlockSpec(shape, lambda b, c: (0,) * len(shape))
    return pl.pallas_call(
        _qkprep_kernel,
        grid=(bsz, nblk),
        in_specs=[
            pl.BlockSpec((tm, ATT_WIDTH), lambda b, c: (b * nblk + c, 1)),
            pl.BlockSpec((tm, ATT_WIDTH), lambda b, c: (b * nblk + c, 2)),
            full((1, ATT_WIDTH)), full((1, ATT_WIDTH)),
            pl.BlockSpec((tm, LANES), lambda b, c: (c, 0)),
            pl.BlockSpec((tm, LANES), lambda b, c: (c, 0)),
            pl.BlockSpec((tm, LANES), lambda b, c: (c, 0)),
            full((256, 256)),
        ],
        out_specs=[
            pl.BlockSpec((tm, ATT_WIDTH), lambda b, c: (b * nblk + c, 0)),
            pl.BlockSpec((1, ATT_WIDTH, tm), lambda b, c: (b, 0, c)),
        ],
        out_shape=[jax.ShapeDtypeStruct((t, ATT_WIDTH), BF16),
                   jax.ShapeDtypeStruct((bsz, ATT_WIDTH, s_len), BF16)],
        compiler_params=_cparams(("parallel", "parallel")),
        name="qk_prep",
    )(proj, proj, qw, kw, cos_t, sa_t, sb_t, gmat)


def _attn_kernel(q_ref, kt_ref, v_ref, lq1_ref, lk1_ref, lq2_ref, lk2_ref, sw_ref, o_ref,
                 m_ref, l_ref, acc_ref, *, tq, tk, lambda_init):
    i = pl.program_id(2)
    q = q_ref[...]
    lane = lax.broadcasted_iota(jnp.int32, q.shape, 1)
    zero = jnp.zeros_like(q)
    qq = jnp.concatenate([jnp.where(lane < ATT_QK_DIM, q, zero), jnp.where(lane >= ATT_QK_DIM, q, zero)], axis=0)

    m_ref[...] = jnp.full(m_ref.shape, -jnp.inf, F32)
    l_ref[...] = jnp.zeros(l_ref.shape, F32)
    acc_ref[...] = jnp.zeros(acc_ref.shape, F32)

    def tile(j, mask):
        c0 = pl.multiple_of(j * tk, tk)
        s = _dot(qq, kt_ref[0, :, pl.ds(c0, tk)])
        if mask is not None:
            s = jnp.where(mask, s, -jnp.inf)
        m_prev = m_ref[...]
        m_new = jnp.maximum(m_prev, jnp.max(s, axis=1, keepdims=True))
        alpha = jnp.exp(m_prev - m_new)
        p = jnp.exp(s - m_new)
        l_ref[...] = alpha * l_ref[...] + jnp.sum(p, axis=1, keepdims=True)
        acc_ref[...] = alpha * acc_ref[...] + _dot(p.astype(BF16), v_ref[pl.ds(c0, tk), :])
        m_ref[...] = m_new

    q_start = i * tq
    n_full = q_start // tk

    def body(j, carry):
        tile(j, None)
        return carry

    lax.fori_loop(0, n_full, body, 0)

    row = lax.broadcasted_iota(jnp.int32, (2 * tq, tk), 0)
    col = lax.broadcasted_iota(jnp.int32, (2 * tq, tk), 1)
    qpos = q_start + jnp.where(row >= tq, row - tq, row)
    kpos = n_full * tk + col
    tile(n_full, (kpos // CHUNK) <= (qpos // CHUNK))

    lam = (jnp.exp(jnp.sum(lq1_ref[...] * lk1_ref[...], axis=1, keepdims=True))
           - jnp.exp(jnp.sum(lq2_ref[...] * lk2_ref[...], axis=1, keepdims=True)) + lambda_init)
    acc = acc_ref[...]
    l = l_ref[...]
    o = acc[0:tq] / l[0:tq] - lam * (acc[tq:2 * tq] / l[tq:2 * tq])
    ms = jnp.mean(o * o, axis=-1, keepdims=True)
    o_ref[...] = (o * lax.rsqrt(ms + EPS) * sw_ref[...] * (1.0 - lambda_init)).astype(o_ref.dtype)


def _attention(q_prep, k_t, proj, lq1, lk1, lq2, lk2, subln_w, lambda_init, bsz, s_len, tq, tk):
    t = q_prep.shape[0]
    nq = s_len // tq
    v_blk0 = (3 * D_MODEL) // ATT_V_DIM
    vec = lambda n: pl.BlockSpec((1, n), lambda b, h, i: (0, 0))
    return pl.pallas_call(
        functools.partial(_attn_kernel, tq=tq, tk=tk, lambda_init=lambda_init),
        grid=(bsz, ATT_HEADS, nq),
        in_specs=[
            pl.BlockSpec((tq, ATT_V_DIM), lambda b, h, i: (b * nq + i, h)),
            pl.BlockSpec((1, ATT_V_DIM, s_len), lambda b, h, i: (b, h, 0)),
            pl.BlockSpec((s_len, ATT_V_DIM), lambda b, h, i: (b, v_blk0 + h)),
            vec(ATT_QK_DIM), vec(ATT_QK_DIM), vec(ATT_QK_DIM), vec(ATT_QK_DIM),
            vec(ATT_V_DIM),
        ],
        out_specs=pl.BlockSpec((tq, ATT_V_DIM), lambda b, h, i: (b * nq + i, h)),
        out_shape=jax.ShapeDtypeStruct((t, ATT_WIDTH), BF16),
        scratch_shapes=[
            pltpu.VMEM((2 * tq, 1), F32),
            pltpu.VMEM((2 * tq, 1), F32),
            pltpu.VMEM((2 * tq, ATT_V_DIM), F32),
        ],
        compiler_params=_cparams(("parallel", "parallel", "arbitrary")),
        name="attn",
    )(q_prep, k_t, proj, lq1, lk1, lq2, lk2, subln_w)


def _merge_kernel(ys_ref, ya_ref, gs_ref, ga_ref, x_ref, gb_ref, ws_ref, wa_ref, wo_ref, nw_ref, *rest,
                  with_router):
    if with_router:
        rw_ref, xo_ref, hn_ref, comb_ref = rest
    else:
        xo_ref, hn_ref = rest
    gs = _sigmoid(gs_ref[...].astype(F32) + gb_ref[0:1, :])
    ga = _sigmoid(ga_ref[...].astype(F32) + gb_ref[1:2, :])
    merged = gs * _dot(ys_ref[...], ws_ref[...]) + ga * _dot(ya_ref[...], wa_ref[...])
    xn = x_ref[...] + _dot(merged.astype(BF16), wo_ref[...])
    xo_ref[...] = xn
    ms = jnp.mean(xn * xn, axis=-1, keepdims=True)
    hn = xn * lax.rsqrt(ms + EPS) * nw_ref[...]
    hn_ref[...] = hn.astype(hn_ref.dtype)
    if with_router:
        h1, h2, _ = _split3(hn)
        logits = _dot(h1, rw_ref[0]) + _dot(h1, rw_ref[1]) + _dot(h2, rw_ref[0])
        lane = lax.broadcasted_iota(jnp.int32, logits.shape, 1)
        neg = jnp.float32(-jnp.inf)
        lg = jnp.where(lane < N_EXPERTS, logits, neg)
        m1 = jnp.max(lg, axis=1, keepdims=True)
        i1 = jnp.min(jnp.where(lg == m1, lane, LANES), axis=1, keepdims=True)
        lg2 = jnp.where(lane == i1, neg, lg)
        m2 = jnp.max(lg2, axis=1, keepdims=True)
        i2 = jnp.min(jnp.where(lg2 == m2, lane, LANES), axis=1, keepdims=True)
        e2 = jnp.exp(m2 - m1)
        w1 = 1.0 / (1.0 + e2)
        w2 = e2 / (1.0 + e2)
        comb_ref[...] = jnp.where(lane == i1, w1, 0.0) + jnp.where(lane == i2, w2, 0.0)


def _merge(y_ssd, y_att, proj, x2d, gate_b, ws, wa, wo, nw, router_w, tm):
    t = x2d.shape[0]
    with_router = router_w is not None
    row = lambda w, blk: pl.BlockSpec((tm, w), lambda i: (i, blk))
    full = lambda shape: pl.BlockSpec(shape, lambda i: (0,) * len(shape))
    in_specs = [
        row(D_MODEL, 0), row(D_MODEL, 0), row(D_MODEL, 4), row(D_MODEL, 5), row(D_MODEL, 0),
        full((2, D_MODEL)), full((D_MODEL, D_MODEL)), full((D_MODEL, D_MODEL)), full((D_MODEL, D_MODEL)),
        full((1, D_MODEL)),
    ]
    args = [y_ssd, y_att, proj, proj, x2d, gate_b, ws, wa, wo, nw]
    out_specs = [row(D_MODEL, 0), row(D_MODEL, 0)]
    out_shape = [jax.ShapeDtypeStruct((t, D_MODEL), F32), jax.ShapeDtypeStruct((t, D_MODEL), BF16)]
    if with_router:
        in_specs.append(full((2, D_MODEL, LANES)))
        args.append(router_w)
        out_specs.append(row(LANES, 0))
        out_shape.append(jax.ShapeDtypeStruct((t, LANES), F32))
    return pl.pallas_call(
        functools.partial(_merge_kernel, with_router=with_router),
        grid=(t // tm,),
        in_specs=in_specs,
        out_specs=out_specs,
        out_shape=out_shape,
        compiler_params=_cparams(("parallel",)),
        name="merge_router" if with_router else "merge",
    )(*args)


def _ffn_kernel(hn_ref, x_ref, wg_ref, wu_ref, wd_ref, o_ref):
    f = pl.program_id(1)
    hn = hn_ref[...]
    h = (_silu(_dot(hn, wg_ref[...])) * _dot(hn, wu_ref[...])).astype(BF16)
    y = _dot(h, wd_ref[...])

    @pl.when(f == 0)
    def _():
        o_ref[...] = x_ref[...] + y

    @pl.when(f > 0)
    def _():
        o_ref[...] += y


def _ffn(hn, x2d, wg, wu, wd, tm, tf):
    t = x2d.shape[0]
    return pl.pallas_call(
        _ffn_kernel,
        grid=(t // tm, D_FF // tf),
        in_specs=[
            pl.BlockSpec((tm, D_MODEL), lambda i, f: (i, 0)),
            pl.BlockSpec((tm, D_MODEL), lambda i, f: (i, 0)),
            pl.BlockSpec((D_MODEL, tf), lambda i, f: (0, f)),
            pl.BlockSpec((D_MODEL, tf), lambda i, f: (0, f)),
            pl.BlockSpec((tf, D_MODEL), lambda i, f: (f, 0)),
        ],
        out_specs=pl.BlockSpec((tm, D_MODEL), lambda i, f: (i, 0)),
        out_shape=jax.ShapeDtypeStruct((t, D_MODEL), F32),
        compiler_params=_cparams(("parallel", "arbitrary")),
        name="ffn",
    )(hn, x2d, wg, wu, wd)


def _moe_kernel(hn_ref, x_ref, comb_ref, wg_ref, wu_ref, wd_ref, o_ref):
    e = pl.program_id(1)
    f = pl.program_id(2)
    hn = hn_ref[...]
    h = (_silu(_dot(hn, wg_ref[0])) * _dot(hn, wu_ref[0])).astype(BF16)
    comb = comb_ref[...]
    lane = lax.broadcasted_iota(jnp.int32, comb.shape, 1)
    w = jnp.sum(jnp.where(lane == e, comb, 0.0), axis=1, keepdims=True)
    y = w * _dot(h, wd_ref[0])

    @pl.when((e == 0) & (f == 0))
    def _():
        o_ref[...] = x_ref[...] + y

    @pl.when((e > 0) | (f > 0))
    def _():
        o_ref[...] += y


def _moe(hn, x2d, comb, wg, wu, wd, tm, tf):
    t = x2d.shape[0]
    return pl.pallas_call(
        _moe_kernel,
        grid=(t // tm, N_EXPERTS, D_FF // tf),
        in_specs=[
            pl.BlockSpec((tm, D_MODEL), lambda i, e, f: (i, 0)),
            pl.BlockSpec((tm, D_MODEL), lambda i, e, f: (i, 0)),
            pl.BlockSpec((tm, LANES), lambda i, e, f: (i, 0)),
            pl.BlockSpec((1, D_MODEL, tf), lambda i, e, f: (e, 0, f)),
            pl.BlockSpec((1, D_MODEL, tf), lambda i, e, f: (e, 0, f)),
            pl.BlockSpec((1, tf, D_MODEL), lambda i, e, f: (e, f, 0)),
        ],
        out_specs=pl.BlockSpec((tm, D_MODEL), lambda i, e, f: (i, 0)),
        out_shape=jax.ShapeDtypeStruct((t, D_MODEL), F32),
        compiler_params=_cparams(("parallel", "arbitrary", "arbitrary")),
        name="moe",
    )(hn, x2d, comb, wg, wu, wd)


def _ssd_consts():
    e_mat = np.zeros((LANES, SSD_INNER), np.float32)
    for h in range(SSD_HEADS):
        e_mat[h, h * SSD_HEAD_DIM:(h + 1) * SSD_HEAD_DIM] = 1.0
    tril = np.tril(np.ones((CHUNK, CHUNK), np.float32))
    dmask = np.tile(np.eye(CHUNK, dtype=np.float32), (1, SSD_HEADS))
    trilmask = np.tile(tril, (1, SSD_HEADS))
    blk = np.arange(256) // SSD_HEAD_DIM
    bdmask = (blk[:, None] == blk[None, :]).astype(np.float32)
    return (jnp.asarray(e_mat, BF16), jnp.asarray(tril, BF16), jnp.asarray(dmask, F32),
            jnp.asarray(trilmask, F32), jnp.asarray(bdmask, BF16))


def _rope_tables(s_len):
    half = ROPE_DIM // 2
    inv = ROPE_THETA ** (-jnp.arange(0, ROPE_DIM, 2, dtype=F32) / ROPE_DIM)
    ang = jnp.arange(s_len, dtype=F32)[:, None] * inv[None, :]
    cos, sin = jnp.cos(ang), jnp.sin(ang)
    pad = jnp.zeros((s_len, ATT_QK_DIM - ROPE_DIM), F32)
    cos_c = jnp.concatenate([cos, cos, pad + 1.0], axis=1)
    sa_c = jnp.concatenate([-sin, jnp.zeros_like(sin), pad], axis=1)
    sb_c = jnp.concatenate([jnp.zeros_like(sin), sin, pad], axis=1)
    two = lambda a: jnp.concatenate([a, a], axis=1)
    return two(cos_c), two(sa_c), two(sb_c)


def _split2_w(w):
    w = jnp.pad(w, ((0, 0), (0, LANES - w.shape[1])))
    hi = w.astype(BF16)
    lo = (w - hi.astype(F32)).astype(BF16)
    return jnp.stack([hi, lo])


def _pick(t, prefs):
    for p in prefs:
        if t % p == 0:
            return p
    return t


def kernel(x, norm_mix_w, w_in, conv_w, conv_b, dt_bias, a_log, d_skip, ssd_norm_w, q_norm_w, k_norm_w,
           lambda_q1, lambda_k1, lambda_q2, lambda_k2, subln_w, gate_b, w_br_ssd, w_br_att, w_out,
           norm_ffn_w, ffn_w_gate, ffn_w_up, ffn_w_down, router_w, moe_w_gate, moe_w_up, moe_w_down):
    bsz, s_len, d = x.shape
    depth = w_in.shape[0]
    t = bsz * s_len
    assert d == D_MODEL and s_len % 512 == 0

    tm_proj = _pick(t, (1024, 512))
    tm_row = _pick(t, (512,))
    lb = 256
    tm_qk = 512
    tq, tk = 256, 512

    consts = _ssd_consts()
    cos_t, sa_t, sb_t = _rope_tables(s_len)
    blk64 = np.arange(256) // ATT_QK_DIM
    gmat = jnp.asarray((blk64[:, None] == blk64[None, :]).astype(np.float32), BF16)

    o1 = SSD_INNER
    o2 = o1 + SSD_CONV_DIM
    o3 = o2 + SSD_HEADS
    o4 = o3 + ATT_WIDTH
    o5 = o4 + ATT_WIDTH
    o6 = o5 + ATT_WIDTH
    o7 = o6 + D_MODEL

    x2d = x.reshape(t, d)
    for i in range(depth):
        lambda_init = 0.8 - 0.6 * math.exp(-0.3 * i)
        wi = w_in[i]
        w_main = jnp.concatenate([wi[:, :o1], wi[:, o3:], wi[:, o1:o2]], axis=1).astype(BF16)
        w_dt = _split2_w(wi[:, o2:o3])
        proj, dt_raw = _in_proj(x2d, norm_mix_w[i][None, :], w_main, w_dt, tm_proj, SSD_CONV_DIM)

        dtb = jnp.pad(dt_bias[i], (0, LANES - SSD_HEADS))[None, :]
        aexp = jnp.repeat(-jnp.exp(a_log[i].astype(F32)), SSD_HEAD_DIM)[None, :]
        dexp = jnp.repeat(d_skip[i], SSD_HEAD_DIM)[None, :]
        y_ssd = _ssd(proj, dt_raw, conv_w[i], conv_b[i][None, :], dtb, aexp, dexp, ssd_norm_w[i][None, :],
                     consts, bsz, s_len, lb)

        qw = jnp.tile(q_norm_w[i], ATT_WIDTH // ATT_QK_DIM)[None, :]
        kw = jnp.tile(k_norm_w[i], ATT_WIDTH // ATT_QK_DIM)[None, :]
        q_prep, k_t = _qk_prep(proj, qw, kw, cos_t, sa_t, sb_t, gmat, bsz, s_len, tm_qk)
        y_att = _attention(q_prep, k_t, proj, lambda_q1[i][None, :], lambda_k1[i][None, :],
                           lambda_q2[i][None, :], lambda_k2[i][None, :], subln_w[i][None, :],
                           lambda_init, bsz, s_len, tq, tk)

        j = i // 2
        is_moe = i % 2 == 1
        rw = _split2_w(router_w[j]) if is_moe else None
        outs = _merge(y_ssd, y_att, proj, x2d, gate_b[i], w_br_ssd[i].astype(BF16), w_br_att[i].astype(BF16),
                      w_out[i].astype(BF16), norm_ffn_w[i][None, :], rw, tm_row)
        if is_moe:
            x2d, hn, comb = outs
            x2d = _moe(hn, x2d, comb, moe_w_gate[j].astype(BF16), moe_w_up[j].astype(BF16),
                       moe_w_down[j].astype(BF16), tm_row, D_FF // 2)
        else:
            x2d, hn = outs
            x2d = _ffn(hn, x2d, ffn_w_gate[j].astype(BF16), ffn_w_up[j].astype(BF16),
                       ffn_w_down[j].astype(BF16), tm_row, D_FF // 2)
    return x2d.reshape(bsz, s_len, d)
```

```python
import functools
import math

import numpy as np
import jax
import jax.numpy as jnp
from jax import lax
from jax.experimental import pallas as pl
from jax.experimental.pallas import tpu as pltpu

F32 = jnp.float32
BF16 = jnp.bfloat16

D_MODEL = 1024
CHUNK = 64
EPS = 1e-6
SSD_HEADS = 16
SSD_HEAD_DIM = 64
SSD_INNER = SSD_HEADS * SSD_HEAD_DIM
SSD_GROUPS = 2
SSD_STATE = 128
SSD_CONV = 4
SSD_CONV_DIM = SSD_INNER + 2 * SSD_GROUPS * SSD_STATE
ATT_HEADS = 8
ATT_QK_DIM = 64
ATT_V_DIM = 128
ATT_WIDTH = ATT_HEADS * ATT_V_DIM
ROPE_THETA = 500000.0
ROPE_DIM = ATT_QK_DIM // 4
D_FF = 2816
N_EXPERTS = 8
LANES = 128
GROUP_W = SSD_INNER // SSD_GROUPS
MAIN_COLS = 6 * D_MODEL + SSD_CONV_DIM
VMEM_LIMIT = 56 * 1024 * 1024


def _cparams(sem):
    return pltpu.CompilerParams(dimension_semantics=sem, vmem_limit_bytes=VMEM_LIMIT)


def _split3(a):
    h1 = a.astype(BF16)
    r1 = a - h1.astype(F32)
    h2 = r1.astype(BF16)
    r2 = r1 - h2.astype(F32)
    return h1, h2, r2.astype(BF16)


def _dot(a, b):
    return jnp.dot(a, b, preferred_element_type=F32)


def _sigmoid(x):
    return 1.0 / (1.0 + jnp.exp(-x))


def _silu(x):
    return x * _sigmoid(x)


def _inproj_kernel(x_ref, nw_ref, w_ref, wdt_ref, o_ref, dt_ref, xn_ref):
    @pl.when(pl.program_id(1) == 0)
    def _():
        x = x_ref[...]
        ms = jnp.mean(x * x, axis=-1, keepdims=True)
        xn = x * lax.rsqrt(ms + EPS) * nw_ref[...]
        x1, x2, _ = _split3(xn)
        xn_ref[...] = x1
        dt_ref[...] = _dot(x1, wdt_ref[0]) + _dot(x1, wdt_ref[1]) + _dot(x2, wdt_ref[0])

    o_ref[...] = _dot(xn_ref[...], w_ref[...]).astype(o_ref.dtype)


def _in_proj(x2d, norm_w, w_main, w_dt, tm, tn):
    t, d = x2d.shape
    n = w_main.shape[1]
    return pl.pallas_call(
        _inproj_kernel,
        grid=(t // tm, n // tn),
        in_specs=[
            pl.BlockSpec((tm, d), lambda i, j: (i, 0)),
            pl.BlockSpec((1, d), lambda i, j: (0, 0)),
            pl.BlockSpec((d, tn), lambda i, j: (0, j)),
            pl.BlockSpec((2, d, LANES), lambda i, j: (0, 0, 0)),
        ],
        out_specs=[
            pl.BlockSpec((tm, tn), lambda i, j: (i, j)),
            pl.BlockSpec((tm, LANES), lambda i, j: (i, 0)),
        ],
        out_shape=[jax.ShapeDtypeStruct((t, n), BF16), jax.ShapeDtypeStruct((t, LANES), F32)],
        scratch_shapes=[pltpu.VMEM((tm, d), BF16)],
        compiler_params=_cparams(("parallel", "arbitrary")),
        name="in_proj",
    )(x2d, norm_w, w_main, w_dt)


def _ssd_kernel(xbc_ref, z_ref, dt_ref, convw_ref, convb_ref, dtb_ref, aexp_ref, dexp_ref, nw_ref,
                e_ref, tril_ref, dmask_ref, trilmask_ref, bdmask_ref,
                o_ref, xpad_ref, u_ref, dte_ref, state_ref, *, lb):
    @pl.when(pl.program_id(1) == 0)
    def _():
        xpad_ref[0:8, :] = jnp.zeros((8, SSD_CONV_DIM), F32)
        state_ref[...] = jnp.zeros_like(state_ref)

    xpad_ref[8:8 + lb, :] = xbc_ref[...].astype(F32)
    acc = convb_ref[...] + convw_ref[0:1, :] * xpad_ref[pl.ds(8 - (SSD_CONV - 1), lb), :]
    for k in range(1, SSD_CONV):
        acc = acc + convw_ref[k:k + 1, :] * xpad_ref[pl.ds(8 - (SSD_CONV - 1) + k, lb), :]
    u_ref[...] = _silu(acc)
    xpad_ref[0:8, :] = xpad_ref[lb:lb + 8, :]

    t = dt_ref[...] + dtb_ref[...]
    dt = jnp.maximum(t, 0.0) + jnp.log(1.0 + jnp.exp(-jnp.abs(t)))
    d1, d2, d3 = _split3(dt)
    e = e_ref[...]
    dte_ref[...] = _dot(d1, e) + _dot(d2, e) + _dot(d3, e)

    tril = tril_ref[...]
    dmask = dmask_ref[...]
    trilmask = trilmask_ref[...] > 0.5
    bdmask = bdmask_ref[...]
    aexp = aexp_ref[...]
    dexp = dexp_ref[...]
    nw = nw_ref[...]

    def chunk(j, carry):
        r0 = pl.multiple_of(j * CHUNK, CHUNK)
        rows = pl.ds(r0, CHUNK)
        xs = u_ref[rows, 0:SSD_INNER]
        bm = u_ref[rows, SSD_INNER:SSD_INNER + SSD_GROUPS * SSD_STATE]
        cm = u_ref[rows, SSD_INNER + SSD_GROUPS * SSD_STATE:SSD_CONV_DIM]
        dte = dte_ref[rows, :]
        a1, a2, a3 = _split3(dte * aexp)
        cs = _dot(tril, a1) + _dot(tril, a2) + _dot(tril, a3)
        rowv = jnp.sum(cs * dmask, axis=0, keepdims=True)
        decay = jnp.exp(jnp.where(trilmask, cs - rowv, -1e30))
        a_end = cs[CHUNK - 1:CHUNK, :]
        to_end = jnp.exp(a_end - cs)
        xdt = xs * dte
        xdt_b = xdt.astype(BF16)
        xw_b = (xdt * to_end).astype(BF16)
        bm_b = bm.astype(BF16)
        cm_b = cm.astype(BF16)

        cb_parts = []
        yoff_parts = []
        for g in range(SSD_GROUPS):
            bg = bm_b[:, g * SSD_STATE:(g + 1) * SSD_STATE]
            cg = cm_b[:, g * SSD_STATE:(g + 1) * SSD_STATE]
            cb = lax.dot_general(cg, bg, (((1,), (1,)), ((), ())), preferred_element_type=F32)
            cb2 = jnp.concatenate([cb, cb], axis=1)
            cb_parts += [cb2] * (GROUP_W // LANES)
            st = state_ref[g]
            yoff_parts.append(_dot(cg, st.astype(BF16)))
            bg_t = jnp.transpose(bm[:, g * SSD_STATE:(g + 1) * SSD_STATE]).astype(BF16)
            upd = _dot(bg_t, xw_b[:, g * GROUP_W:(g + 1) * GROUP_W])
            state_ref[g] = jnp.exp(a_end[:, g * GROUP_W:(g + 1) * GROUP_W]) * st + upd
        m_b = (jnp.concatenate(cb_parts, axis=1) * decay).astype(BF16)
        y_off = jnp.concatenate(yoff_parts, axis=1) * jnp.exp(cs)

        yd_parts = []
        for q in range(SSD_INNER // 256):
            sl = slice(q * 256, (q + 1) * 256)
            rhs = jnp.concatenate([xdt_b[:, sl]] * 4, axis=0) * bdmask
            yd_parts.append(_dot(m_b[:, sl], rhs))
        y = jnp.concatenate(yd_parts, axis=1) + y_off + xs * dexp

        yz = y * _silu(z_ref[rows, :].astype(F32))
        outs = []
        for g in range(SSD_GROUPS):
            yg = yz[:, g * GROUP_W:(g + 1) * GROUP_W]
            ms = jnp.mean(yg * yg, axis=-1, keepdims=True)
            outs.append(yg * lax.rsqrt(ms + EPS))
        o_ref[rows, :] = (jnp.concatenate(outs, axis=1) * nw).astype(o_ref.dtype)
        return carry

    lax.fori_loop(0, lb // CHUNK, chunk, 0)


def _ssd(proj, dt_raw, conv_w, conv_b, dtb, aexp, dexp, norm_w, consts, bsz, s_len, lb):
    t = proj.shape[0]
    nblk = s_len // lb
    xbc_blk0 = (6 * D_MODEL) // SSD_CONV_DIM
    full = lambda shape: pl.BlockSpec(shape, lambda b, c: (0,) * len(shape))
    e_mat, tril, dmask, trilmask, bdmask = consts
    return pl.pallas_call(
        functools.partial(_ssd_kernel, lb=lb),
        grid=(bsz, nblk),
        in_specs=[
            pl.BlockSpec((lb, SSD_CONV_DIM), lambda b, c: (b * nblk + c, xbc_blk0)),
            pl.BlockSpec((lb, SSD_INNER), lambda b, c: (b * nblk + c, 0)),
            pl.BlockSpec((lb, LANES), lambda b, c: (b * nblk + c, 0)),
            full((SSD_CONV, SSD_CONV_DIM)),
            full((1, SSD_CONV_DIM)),
            full((1, LANES)),
            full((1, SSD_INNER)),
            full((1, SSD_INNER)),
            full((1, SSD_INNER)),
            full(e_mat.shape), full(tril.shape), full(dmask.shape), full(trilmask.shape), full(bdmask.shape),
        ],
        out_specs=pl.BlockSpec((lb, SSD_INNER), lambda b, c: (b * nblk + c, 0)),
        out_shape=jax.ShapeDtypeStruct((t, SSD_INNER), BF16),
        scratch_shapes=[
            pltpu.VMEM((lb + 8, SSD_CONV_DIM), F32),
            pltpu.VMEM((lb, SSD_CONV_DIM), F32),
            pltpu.VMEM((lb, SSD_INNER), F32),
            pltpu.VMEM((SSD_GROUPS, SSD_STATE, GROUP_W), F32),
        ],
        compiler_params=_cparams(("parallel", "arbitrary")),
        name="ssd",
    )(proj, proj, dt_raw, conv_w, conv_b, dtb, aexp, dexp, norm_w, e_mat, tril, dmask, trilmask, bdmask)


def _qkprep_kernel(q_ref, k_ref, qw_ref, kw_ref, cos_ref, sa_ref, sb_ref, g_ref, qo_ref, kto_ref):
    rep = ATT_WIDTH // LANES
    cos = jnp.concatenate([cos_ref[...]] * rep, axis=1)
    sa = jnp.concatenate([sa_ref[...]] * rep, axis=1)
    sb = jnp.concatenate([sb_ref[...]] * rep, axis=1)
    gmat = g_ref[...]

    def prep(x, w):
        sq = (x * x).astype(BF16)
        ss = jnp.concatenate([_dot(sq[:, c * 256:(c + 1) * 256], gmat) for c in range(ATT_WIDTH // 256)], axis=1)
        xn = x * lax.rsqrt(ss * (1.0 / ATT_QK_DIM) + EPS) * w
        half = ROPE_DIM // 2
        slabs = [xn[:, c * LANES:(c + 1) * LANES] for c in range(rep)]
        up = jnp.concatenate([pltpu.roll(s, LANES - half, 1) for s in slabs], axis=1)
        dn = jnp.concatenate([pltpu.roll(s, half, 1) for s in slabs], axis=1)
        return xn * cos + up * sa + dn * sb

    q = prep(q_ref[...].astype(F32), qw_ref[...]) * (ATT_QK_DIM ** -0.5 * math.log2(math.e))
    qo_ref[...] = q.astype(qo_ref.dtype)
    k = prep(k_ref[...].astype(F32), kw_ref[...])
    kto_ref[0] = jnp.transpose(k).astype(kto_ref.dtype)


def _qk_prep(proj, qw, kw, cos_t, sa_t, sb_t, gmat, bsz, s_len, tm):
    t = proj.shape[0]
    nblk = s_len // tm
    full = lambda shape: pl.BlockSpec(shape, lambda b, c: (0,) * len(shape))
    return pl.pallas_call(
        _qkprep_kernel,
        grid=(bsz, nblk),
        in_specs=[
            pl.BlockSpec((tm, ATT_WIDTH), lambda b, c: (b * nblk + c, 1)),
            pl.BlockSpec((tm, ATT_WIDTH), lambda b, c: (b * nblk + c, 2)),
            full((1, ATT_WIDTH)), full((1, ATT_WIDTH)),
            pl.BlockSpec((tm, LANES), lambda b, c: (c, 0)),
            pl.BlockSpec((tm, LANES), lambda b, c: (c, 0)),
            pl.BlockSpec((tm, LANES), lambda b, c: (c, 0)),
            full((256, 256)),
        ],
        out_specs=[
            pl.BlockSpec((tm, ATT_WIDTH), lambda b, c: (b * nblk + c, 0)),
            pl.BlockSpec((1, ATT_WIDTH, tm), lambda b, c: (b, 0, c)),
        ],
        out_shape=[jax.ShapeDtypeStruct((t, ATT_WIDTH), BF16),
                   jax.ShapeDtypeStruct((bsz, ATT_WIDTH, s_len), BF16)],
        compiler_params=_cparams(("parallel", "parallel")),
        name="qk_prep",
    )(proj, proj, qw, kw, cos_t, sa_t, sb_t, gmat)


def _attn_kernel(q_ref, kt_ref, v_ref, lq1_ref, lk1_ref, lq2_ref, lk2_ref, sw_ref, o_ref,
                 m_ref, l_ref, acc_ref, *, tq, hp, lambda_init):
    i = pl.program_id(2)
    w = ATT_V_DIM
    lane = lax.broadcasted_iota(jnp.int32, (tq, w), 1)
    qqs = []
    for h in range(hp):
        q = q_ref[:, h * w:(h + 1) * w]
        zero = jnp.zeros_like(q)
        qqs.append(jnp.concatenate([jnp.where(lane < ATT_QK_DIM, q, zero),
                                    jnp.where(lane >= ATT_QK_DIM, q, zero)], axis=0))

    m_ref[...] = jnp.full(m_ref.shape, -jnp.inf, F32)
    l_ref[...] = jnp.zeros(l_ref.shape, F32)
    acc_ref[...] = jnp.zeros(acc_ref.shape, F32)

    def tile(j, mask):
        c0 = pl.multiple_of(j * tq, tq)
        for h in range(hp):
            s = _dot(qqs[h], kt_ref[0, h * w:(h + 1) * w, pl.ds(c0, tq)])
            if mask is not None:
                s = jnp.where(mask, s, -jnp.inf)
            m_prev = m_ref[h]
            m_new = jnp.maximum(m_prev, jnp.max(s, axis=1, keepdims=True))
            alpha = jnp.exp2(m_prev - m_new)
            p = jnp.exp2(s - jnp.concatenate([m_new] * (tq // LANES), axis=1))
            psum = p[:, 0:LANES]
            for c in range(1, tq // LANES):
                psum = psum + p[:, c * LANES:(c + 1) * LANES]
            l_ref[h] = alpha * l_ref[h] + psum
            acc_ref[h] = alpha * acc_ref[h] + _dot(p.astype(BF16), v_ref[pl.ds(c0, tq), h * w:(h + 1) * w])
            m_ref[h] = m_new

    def body(j, carry):
        tile(j, None)
        return carry

    lax.fori_loop(0, i, body, 0)

    row = lax.broadcasted_iota(jnp.int32, (2 * tq, tq), 0)
    col = lax.broadcasted_iota(jnp.int32, (2 * tq, tq), 1)
    row = jnp.where(row >= tq, row - tq, row)
    tile(i, (col // CHUNK) <= (row // CHUNK))

    lam = (jnp.exp(jnp.sum(lq1_ref[...] * lk1_ref[...], axis=1, keepdims=True))
           - jnp.exp(jnp.sum(lq2_ref[...] * lk2_ref[...], axis=1, keepdims=True)) + lambda_init)
    for h in range(hp):
        acc = acc_ref[h]
        l = jnp.sum(l_ref[h], axis=1, keepdims=True)
        o = acc[0:tq] / l[0:tq] - lam * (acc[tq:2 * tq] / l[tq:2 * tq])
        ms = jnp.mean(o * o, axis=-1, keepdims=True)
        o_ref[:, h * w:(h + 1) * w] = (o * lax.rsqrt(ms + EPS) * sw_ref[...] * (1.0 - lambda_init)).astype(o_ref.dtype)


def _attention(q_prep, k_t, proj, lq1, lk1, lq2, lk2, subln_w, lambda_init, bsz, s_len, tq, hp):
    t = q_prep.shape[0]
    nq = s_len // tq
    wblk = hp * ATT_V_DIM
    v_blk0 = (3 * D_MODEL) // wblk
    vec = lambda n: pl.BlockSpec((1, n), lambda b, h, i: (0, 0))
    return pl.pallas_call(
        functools.partial(_attn_kernel, tq=tq, hp=hp, lambda_init=lambda_init),
        grid=(bsz, ATT_HEADS // hp, nq),
        in_specs=[
            pl.BlockSpec((tq, wblk), lambda b, h, i: (b * nq + i, h)),
            pl.BlockSpec((1, wblk, s_len), lambda b, h, i: (b, h, 0)),
            pl.BlockSpec((s_len, wblk), lambda b, h, i: (b, v_blk0 + h)),
            vec(ATT_QK_DIM), vec(ATT_QK_DIM), vec(ATT_QK_DIM), vec(ATT_QK_DIM),
            vec(ATT_V_DIM),
        ],
        out_specs=pl.BlockSpec((tq, wblk), lambda b, h, i: (b * nq + i, h)),
        out_shape=jax.ShapeDtypeStruct((t, ATT_WIDTH), BF16),
        scratch_shapes=[
            pltpu.VMEM((hp, 2 * tq, LANES), F32),
            pltpu.VMEM((hp, 2 * tq, LANES), F32),
            pltpu.VMEM((hp, 2 * tq, ATT_V_DIM), F32),
        ],
        compiler_params=_cparams(("parallel", "parallel", "arbitrary")),
        name="attn",
    )(q_prep, k_t, proj, lq1, lk1, lq2, lk2, subln_w)


def _merge_kernel(ys_ref, ya_ref, gs_ref, ga_ref, x_ref, gb_ref, ws_ref, wa_ref, wo_ref, nw_ref, *rest,
                  with_router):
    if with_router:
        rw_ref, xo_ref, hn_ref, comb_ref = rest
    else:
        xo_ref, hn_ref = rest
    gs = _sigmoid(gs_ref[...].astype(F32) + gb_ref[0:1, :])
    ga = _sigmoid(ga_ref[...].astype(F32) + gb_ref[1:2, :])
    merged = gs * _dot(ys_ref[...], ws_ref[...]) + ga * _dot(ya_ref[...], wa_ref[...])
    xn = x_ref[...] + _dot(merged.astype(BF16), wo_ref[...])
    xo_ref[...] = xn
    ms = jnp.mean(xn * xn, axis=-1, keepdims=True)
    hn = xn * lax.rsqrt(ms + EPS) * nw_ref[...]
    hn_ref[...] = hn.astype(hn_ref.dtype)
    if with_router:
        h1, h2, _ = _split3(hn)
        logits = _dot(h1, rw_ref[0]) + _dot(h1, rw_ref[1]) + _dot(h2, rw_ref[0])
        lane = lax.broadcasted_iota(jnp.int32, logits.shape, 1)
        neg = jnp.float32(-jnp.inf)
        lg = jnp.where(lane < N_EXPERTS, logits, neg)
        m1 = jnp.max(lg, axis=1, keepdims=True)
        i1 = jnp.min(jnp.where(lg == m1, lane, LANES), axis=1, keepdims=True)
        lg2 = jnp.where(lane == i1, neg, lg)
        m2 = jnp.max(lg2, axis=1, keepdims=True)
        i2 = jnp.min(jnp.where(lg2 == m2, lane, LANES), axis=1, keepdims=True)
        e2 = jnp.exp(m2 - m1)
        w1 = 1.0 / (1.0 + e2)
        w2 = e2 / (1.0 + e2)
        comb_ref[...] = jnp.where(lane == i1, w1, 0.0) + jnp.where(lane == i2, w2, 0.0)


def _merge(y_ssd, y_att, proj, x2d, gate_b, ws, wa, wo, nw, router_w, tm):
    t = x2d.shape[0]
    with_router = router_w is not None
    row = lambda w, blk: pl.BlockSpec((tm, w), lambda i: (i, blk))
    full = lambda shape: pl.BlockSpec(shape, lambda i: (0,) * len(shape))
    in_specs = [
        row(D_MODEL, 0), row(D_MODEL, 0), row(D_MODEL, 4), row(D_MODEL, 5), row(D_MODEL, 0),
        full((2, D_MODEL)), full((D_MODEL, D_MODEL)), full((D_MODEL, D_MODEL)), full((D_MODEL, D_MODEL)),
        full((1, D_MODEL)),
    ]
    args = [y_ssd, y_att, proj, proj, x2d, gate_b, ws, wa, wo, nw]
    out_specs = [row(D_MODEL, 0), row(D_MODEL, 0)]
    out_shape = [jax.ShapeDtypeStruct((t, D_MODEL), F32), jax.ShapeDtypeStruct((t, D_MODEL), BF16)]
    if with_router:
        in_specs.append(full((2, D_MODEL, LANES)))
        args.append(router_w)
        out_specs.append(row(LANES, 0))
        out_shape.append(jax.ShapeDtypeStruct((t, LANES), F32))
    return pl.pallas_call(
        functools.partial(_merge_kernel, with_router=with_router),
        grid=(t // tm,),
        in_specs=in_specs,
        out_specs=out_specs,
        out_shape=out_shape,
        compiler_params=_cparams(("parallel",)),
        name="merge_router" if with_router else "merge",
    )(*args)


def _ffn_kernel(hn_ref, x_ref, wg_ref, wu_ref, wd_ref, o_ref):
    f = pl.program_id(1)
    hn = hn_ref[...]
    h = (_silu(_dot(hn, wg_ref[...])) * _dot(hn, wu_ref[...])).astype(BF16)
    y = _dot(h, wd_ref[...])

    @pl.when(f == 0)
    def _():
        o_ref[...] = x_ref[...] + y

    @pl.when(f > 0)
    def _():
        o_ref[...] += y


def _ffn(hn, x2d, wg, wu, wd, tm, tf):
    t = x2d.shape[0]
    return pl.pallas_call(
        _ffn_kernel,
        grid=(t // tm, D_FF // tf),
        in_specs=[
            pl.BlockSpec((tm, D_MODEL), lambda i, f: (i, 0)),
            pl.BlockSpec((tm, D_MODEL), lambda i, f: (i, 0)),
            pl.BlockSpec((D_MODEL, tf), lambda i, f: (0, f)),
            pl.BlockSpec((D_MODEL, tf), lambda i, f: (0, f)),
            pl.BlockSpec((tf, D_MODEL), lambda i, f: (f, 0)),
        ],
        out_specs=pl.BlockSpec((tm, D_MODEL), lambda i, f: (i, 0)),
        out_shape=jax.ShapeDtypeStruct((t, D_MODEL), F32),
        compiler_params=_cparams(("parallel", "arbitrary")),
        name="ffn",
    )(hn, x2d, wg, wu, wd)


def _moe_kernel(hn_ref, x_ref, comb_ref, wg_ref, wu_ref, wd_ref, o_ref):
    e = pl.program_id(1)
    f = pl.program_id(2)
    hn = hn_ref[...]
    h = (_silu(_dot(hn, wg_ref[0])) * _dot(hn, wu_ref[0])).astype(BF16)
    comb = comb_ref[...]
    lane = lax.broadcasted_iota(jnp.int32, comb.shape, 1)
    w = jnp.sum(jnp.where(lane == e, comb, 0.0), axis=1, keepdims=True)
    y = w * _dot(h, wd_ref[0])

    @pl.when((e == 0) & (f == 0))
    def _():
        o_ref[...] = x_ref[...] + y

    @pl.when((e > 0) | (f > 0))
    def _():
        o_ref[...] += y


def _moe(hn, x2d, comb, wg, wu, wd, tm, tf):
    t = x2d.shape[0]
    return pl.pallas_call(
        _moe_kernel,
        grid=(t // tm, N_EXPERTS, D_FF // tf),
        in_specs=[
            pl.BlockSpec((tm, D_MODEL), lambda i, e, f: (i, 0)),
            pl.BlockSpec((tm, D_MODEL), lambda i, e, f: (i, 0)),
            pl.BlockSpec((tm, LANES), lambda i, e, f: (i, 0)),
            pl.BlockSpec((1, D_MODEL, tf), lambda i, e, f: (e, 0, f)),
            pl.BlockSpec((1, D_MODEL, tf), lambda i, e, f: (e, 0, f)),
            pl.BlockSpec((1, tf, D_MODEL), lambda i, e, f: (e, f, 0)),
        ],
        out_specs=pl.BlockSpec((tm, D_MODEL), lambda i, e, f: (i, 0)),
        out_shape=jax.ShapeDtypeStruct((t, D_MODEL), F32),
        compiler_params=_cparams(("parallel", "arbitrary", "arbitrary")),
        name="moe",
    )(hn, x2d, comb, wg, wu, wd)


def _ssd_consts():
    e_mat = np.zeros((LANES, SSD_INNER), np.float32)
    for h in range(SSD_HEADS):
        e_mat[h, h * SSD_HEAD_DIM:(h + 1) * SSD_HEAD_DIM] = 1.0
    tril = np.tril(np.ones((CHUNK, CHUNK), np.float32))
    dmask = np.tile(np.eye(CHUNK, dtype=np.float32), (1, SSD_HEADS))
    trilmask = np.tile(tril, (1, SSD_HEADS))
    blk = np.arange(256) // SSD_HEAD_DIM
    bdmask = (blk[:, None] == blk[None, :]).astype(np.float32)
    return (jnp.asarray(e_mat, BF16), jnp.asarray(tril, BF16), jnp.asarray(dmask, F32),
            jnp.asarray(trilmask, F32), jnp.asarray(bdmask, BF16))


def _rope_tables(s_len):
    half = ROPE_DIM // 2
    inv = ROPE_THETA ** (-jnp.arange(0, ROPE_DIM, 2, dtype=F32) / ROPE_DIM)
    ang = jnp.arange(s_len, dtype=F32)[:, None] * inv[None, :]
    cos, sin = jnp.cos(ang), jnp.sin(ang)
    pad = jnp.zeros((s_len, ATT_QK_DIM - ROPE_DIM), F32)
    cos_c = jnp.concatenate([cos, cos, pad + 1.0], axis=1)
    sa_c = jnp.concatenate([-sin, jnp.zeros_like(sin), pad], axis=1)
    sb_c = jnp.concatenate([jnp.zeros_like(sin), sin, pad], axis=1)
    two = lambda a: jnp.concatenate([a, a], axis=1)
    return two(cos_c), two(sa_c), two(sb_c)


def _split2_w(w):
    w = jnp.pad(w, ((0, 0), (0, LANES - w.shape[1])))
    hi = w.astype(BF16)
    lo = (w - hi.astype(F32)).astype(BF16)
    return jnp.stack([hi, lo])


def _pick(t, prefs):
    for p in prefs:
        if t % p == 0:
            return p
    return t


def kernel(x, norm_mix_w, w_in, conv_w, conv_b, dt_bias, a_log, d_skip, ssd_norm_w, q_norm_w, k_norm_w,
           lambda_q1, lambda_k1, lambda_q2, lambda_k2, subln_w, gate_b, w_br_ssd, w_br_att, w_out,
           norm_ffn_w, ffn_w_gate, ffn_w_up, ffn_w_down, router_w, moe_w_gate, moe_w_up, moe_w_down):
    bsz, s_len, d = x.shape
    depth = w_in.shape[0]
    t = bsz * s_len
    assert d == D_MODEL and s_len % 512 == 0

    tm_proj = _pick(t, (1024, 512))
    tm_row = _pick(t, (512,))
    lb = 256
    tm_qk = 512
    tq, heads_per_step = 512, 2

    consts = _ssd_consts()
    cos_t, sa_t, sb_t = _rope_tables(s_len)
    blk64 = np.arange(256) // ATT_QK_DIM
    gmat = jnp.asarray((blk64[:, None] == blk64[None, :]).astype(np.float32), BF16)

    o1 = SSD_INNER
    o2 = o1 + SSD_CONV_DIM
    o3 = o2 + SSD_HEADS
    o4 = o3 + ATT_WIDTH
    o5 = o4 + ATT_WIDTH
    o6 = o5 + ATT_WIDTH
    o7 = o6 + D_MODEL

    x2d = x.reshape(t, d)
    for i in range(depth):
        lambda_init = 0.8 - 0.6 * math.exp(-0.3 * i)
        wi = w_in[i]
        w_main = jnp.concatenate([wi[:, :o1], wi[:, o3:], wi[:, o1:o2]], axis=1).astype(BF16)
        w_dt = _split2_w(wi[:, o2:o3])
        proj, dt_raw = _in_proj(x2d, norm_mix_w[i][None, :], w_main, w_dt, tm_proj, SSD_CONV_DIM)

        dtb = jnp.pad(dt_bias[i], (0, LANES - SSD_HEADS))[None, :]
        aexp = jnp.repeat(-jnp.exp(a_log[i].astype(F32)), SSD_HEAD_DIM)[None, :]
        dexp = jnp.repeat(d_skip[i], SSD_HEAD_DIM)[None, :]
        y_ssd = _ssd(proj, dt_raw, conv_w[i], conv_b[i][None, :], dtb, aexp, dexp, ssd_norm_w[i][None, :],
                     consts, bsz, s_len, lb)

        qw = jnp.tile(q_norm_w[i], ATT_WIDTH // ATT_QK_DIM)[None, :]
        kw = jnp.tile(k_norm_w[i], ATT_WIDTH // ATT_QK_DIM)[None, :]
        q_prep, k_t = _qk_prep(proj, qw, kw, cos_t, sa_t, sb_t, gmat, bsz, s_len, tm_qk)
        y_att = _attention(q_prep, k_t, proj, lambda_q1[i][None, :], lambda_k1[i][None, :],
                           lambda_q2[i][None, :], lambda_k2[i][None, :], subln_w[i][None, :],
                           lambda_init, bsz, s_len, tq, heads_per_step)

        j = i // 2
        is_moe = i % 2 == 1
        rw = _split2_w(router_w[j]) if is_moe else None
        outs = _merge(y_ssd, y_att, proj, x2d, gate_b[i], w_br_ssd[i].astype(BF16), w_br_att[i].astype(BF16),
                      w_out[i].astype(BF16), norm_ffn_w[i][None, :], rw, tm_row)
        if is_moe:
            x2d, hn, comb = outs
            x2d = _moe(hn, x2d, comb, moe_w_gate[j].astype(BF16), moe_w_up[j].astype(BF16),
                       moe_w_down[j].astype(BF16), tm_row, D_FF // 2)
        else:
            x2d, hn = outs
            x2d = _ffn(hn, x2d, ffn_w_gate[j].astype(BF16), ffn_w_up[j].astype(BF16),
                       ffn_w_down[j].astype(BF16), tm_row, D_FF // 2)
    return x2d.reshape(bsz, s_len, d)
```

```python
import functools
import math

import numpy as np
import jax
import jax.numpy as jnp
from jax import lax
from jax.experimental import pallas as pl
from jax.experimental.pallas import tpu as pltpu

F32 = jnp.float32
BF16 = jnp.bfloat16

D_MODEL = 1024
CHUNK = 64
EPS = 1e-6
SSD_HEADS = 16
SSD_HEAD_DIM = 64
SSD_INNER = SSD_HEADS * SSD_HEAD_DIM
SSD_GROUPS = 2
SSD_STATE = 128
SSD_CONV = 4
SSD_CONV_DIM = SSD_INNER + 2 * SSD_GROUPS * SSD_STATE
ATT_HEADS = 8
ATT_QK_DIM = 64
ATT_V_DIM = 128
ATT_WIDTH = ATT_HEADS * ATT_V_DIM
ROPE_THETA = 500000.0
ROPE_DIM = ATT_QK_DIM // 4
D_FF = 2816
N_EXPERTS = 8
MOE_TILE = 512
LANES = 128
GROUP_W = SSD_INNER // SSD_GROUPS
MAIN_COLS = 6 * D_MODEL + SSD_CONV_DIM
VMEM_LIMIT = 56 * 1024 * 1024


def _cparams(sem):
    return pltpu.CompilerParams(dimension_semantics=sem, vmem_limit_bytes=VMEM_LIMIT)


def _split3(a):
    h1 = a.astype(BF16)
    r1 = a - h1.astype(F32)
    h2 = r1.astype(BF16)
    r2 = r1 - h2.astype(F32)
    return h1, h2, r2.astype(BF16)


def _dot(a, b):
    return jnp.dot(a, b, preferred_element_type=F32)


def _sigmoid(x):
    return 1.0 / (1.0 + jnp.exp(-x))


def _silu(x):
    return x * _sigmoid(x)


def _inproj_kernel(x_ref, nw_ref, w_ref, wdt_ref, o_ref, dt_ref, xn_ref):
    @pl.when(pl.program_id(1) == 0)
    def _():
        x = x_ref[...]
        ms = jnp.mean(x * x, axis=-1, keepdims=True)
        xn = x * lax.rsqrt(ms + EPS) * nw_ref[...]
        x1, x2, _ = _split3(xn)
        xn_ref[...] = x1
        dt_ref[...] = _dot(x1, wdt_ref[0]) + _dot(x1, wdt_ref[1]) + _dot(x2, wdt_ref[0])

    o_ref[...] = _dot(xn_ref[...], w_ref[...]).astype(o_ref.dtype)


def _in_proj(x2d, norm_w, w_main, w_dt, tm, tn):
    t, d = x2d.shape
    n = w_main.shape[1]
    return pl.pallas_call(
        _inproj_kernel,
        grid=(t // tm, n // tn),
        in_specs=[
            pl.BlockSpec((tm, d), lambda i, j: (i, 0)),
            pl.BlockSpec((1, d), lambda i, j: (0, 0)),
            pl.BlockSpec((d, tn), lambda i, j: (0, j)),
            pl.BlockSpec((2, d, LANES), lambda i, j: (0, 0, 0)),
        ],
        out_specs=[
            pl.BlockSpec((tm, tn), lambda i, j: (i, j)),
            pl.BlockSpec((tm, LANES), lambda i, j: (i, 0)),
        ],
        out_shape=[jax.ShapeDtypeStruct((t, n), BF16), jax.ShapeDtypeStruct((t, LANES), F32)],
        scratch_shapes=[pltpu.VMEM((tm, d), BF16)],
        compiler_params=_cparams(("parallel", "arbitrary")),
        name="in_proj",
    )(x2d, norm_w, w_main, w_dt)


def _ssd_kernel(xbc_ref, z_ref, dt_ref, convw_ref, convb_ref, dtb_ref, aexp_ref, dexp_ref, nw_ref,
                e_ref, tril_ref, dmask_ref, trilmask_ref, bdmask_ref,
                o_ref, xpad_ref, u_ref, dte_ref, state_ref, *, lb):
    @pl.when(pl.program_id(1) == 0)
    def _():
        xpad_ref[0:8, :] = jnp.zeros((8, SSD_CONV_DIM), F32)
        state_ref[...] = jnp.zeros_like(state_ref)

    xpad_ref[8:8 + lb, :] = xbc_ref[...].astype(F32)
    acc = convb_ref[...] + convw_ref[0:1, :] * xpad_ref[pl.ds(8 - (SSD_CONV - 1), lb), :]
    for k in range(1, SSD_CONV):
        acc = acc + convw_ref[k:k + 1, :] * xpad_ref[pl.ds(8 - (SSD_CONV - 1) + k, lb), :]
    u_ref[...] = _silu(acc)
    xpad_ref[0:8, :] = xpad_ref[lb:lb + 8, :]

    t = dt_ref[...] + dtb_ref[...]
    dt = jnp.maximum(t, 0.0) + jnp.log(1.0 + jnp.exp(-jnp.abs(t)))
    d1, d2, d3 = _split3(dt)
    e = e_ref[...]
    dte_ref[...] = _dot(d1, e) + _dot(d2, e) + _dot(d3, e)

    tril = tril_ref[...]
    dmask = dmask_ref[...]
    trilmask = trilmask_ref[...] > 0.5
    bdmask = bdmask_ref[...]
    aexp = aexp_ref[...]
    dexp = dexp_ref[...]
    nw = nw_ref[...]

    def chunk(j, carry):
        r0 = pl.multiple_of(j * CHUNK, CHUNK)
        rows = pl.ds(r0, CHUNK)
        xs = u_ref[rows, 0:SSD_INNER]
        bm = u_ref[rows, SSD_INNER:SSD_INNER + SSD_GROUPS * SSD_STATE]
        cm = u_ref[rows, SSD_INNER + SSD_GROUPS * SSD_STATE:SSD_CONV_DIM]
        dte = dte_ref[rows, :]
        a1, a2, a3 = _split3(dte * aexp)
        cs = _dot(tril, a1) + _dot(tril, a2) + _dot(tril, a3)
        rowv = jnp.sum(cs * dmask, axis=0, keepdims=True)
        decay = jnp.exp(jnp.where(trilmask, cs - rowv, -1e30))
        a_end = cs[CHUNK - 1:CHUNK, :]
        to_end = jnp.exp(a_end - cs)
        xdt = xs * dte
        xdt_b = xdt.astype(BF16)
        xw_b = (xdt * to_end).astype(BF16)
        bm_b = bm.astype(BF16)
        cm_b = cm.astype(BF16)

        cb_parts = []
        yoff_parts = []
        for g in range(SSD_GROUPS):
            bg = bm_b[:, g * SSD_STATE:(g + 1) * SSD_STATE]
            cg = cm_b[:, g * SSD_STATE:(g + 1) * SSD_STATE]
            cb = lax.dot_general(cg, bg, (((1,), (1,)), ((), ())), preferred_element_type=F32)
            cb2 = jnp.concatenate([cb, cb], axis=1)
            cb_parts += [cb2] * (GROUP_W // LANES)
            st = state_ref[g]
            yoff_parts.append(_dot(cg, st.astype(BF16)))
            bg_t = jnp.transpose(bm[:, g * SSD_STATE:(g + 1) * SSD_STATE]).astype(BF16)
            upd = _dot(bg_t, xw_b[:, g * GROUP_W:(g + 1) * GROUP_W])
            state_ref[g] = jnp.exp(a_end[:, g * GROUP_W:(g + 1) * GROUP_W]) * st + upd
        m_b = (jnp.concatenate(cb_parts, axis=1) * decay).astype(BF16)
        y_off = jnp.concatenate(yoff_parts, axis=1) * jnp.exp(cs)

        yd_parts = []
        for q in range(SSD_INNER // 256):
            sl = slice(q * 256, (q + 1) * 256)
            rhs = jnp.concatenate([xdt_b[:, sl]] * 4, axis=0) * bdmask
            yd_parts.append(_dot(m_b[:, sl], rhs))
        y = jnp.concatenate(yd_parts, axis=1) + y_off + xs * dexp

        yz = y * _silu(z_ref[rows, :].astype(F32))
        outs = []
        for g in range(SSD_GROUPS):
            yg = yz[:, g * GROUP_W:(g + 1) * GROUP_W]
            ms = jnp.mean(yg * yg, axis=-1, keepdims=True)
            outs.append(yg * lax.rsqrt(ms + EPS))
        o_ref[rows, :] = (jnp.concatenate(outs, axis=1) * nw).astype(o_ref.dtype)
        return carry

    lax.fori_loop(0, lb // CHUNK, chunk, 0)


def _ssd(proj, dt_raw, conv_w, conv_b, dtb, aexp, dexp, norm_w, consts, bsz, s_len, lb):
    t = proj.shape[0]
    nblk = s_len // lb
    xbc_blk0 = (6 * D_MODEL) // SSD_CONV_DIM
    full = lambda shape: pl.BlockSpec(shape, lambda b, c: (0,) * len(shape))
    e_mat, tril, dmask, trilmask, bdmask = consts
    return pl.pallas_call(
        functools.partial(_ssd_kernel, lb=lb),
        grid=(bsz, nblk),
        in_specs=[
            pl.BlockSpec((lb, SSD_CONV_DIM), lambda b, c: (b * nblk + c, xbc_blk0)),
            pl.BlockSpec((lb, SSD_INNER), lambda b, c: (b * nblk + c, 0)),
            pl.BlockSpec((lb, LANES), lambda b, c: (b * nblk + c, 0)),
            full((SSD_CONV, SSD_CONV_DIM)),
            full((1, SSD_CONV_DIM)),
            full((1, LANES)),
            full((1, SSD_INNER)),
            full((1, SSD_INNER)),
            full((1, SSD_INNER)),
            full(e_mat.shape), full(tril.shape), full(dmask.shape), full(trilmask.shape), full(bdmask.shape),
        ],
        out_specs=pl.BlockSpec((lb, SSD_INNER), lambda b, c: (b * nblk + c, 0)),
        out_shape=jax.ShapeDtypeStruct((t, SSD_INNER), BF16),
        scratch_shapes=[
            pltpu.VMEM((lb + 8, SSD_CONV_DIM), F32),
            pltpu.VMEM((lb, SSD_CONV_DIM), F32),
            pltpu.VMEM((lb, SSD_INNER), F32),
            pltpu.VMEM((SSD_GROUPS, SSD_STATE, GROUP_W), F32),
        ],
        compiler_params=_cparams(("parallel", "arbitrary")),
        name="ssd",
    )(proj, proj, dt_raw, conv_w, conv_b, dtb, aexp, dexp, norm_w, e_mat, tril, dmask, trilmask, bdmask)


def _qkprep_kernel(q_ref, k_ref, qw_ref, kw_ref, cos_ref, sa_ref, sb_ref, g_ref, qo_ref, kto_ref):
    rep = ATT_WIDTH // LANES
    cos = jnp.concatenate([cos_ref[...]] * rep, axis=1)
    sa = jnp.concatenate([sa_ref[...]] * rep, axis=1)
    sb = jnp.concatenate([sb_ref[...]] * rep, axis=1)
    gmat = g_ref[...]

    def prep(x, w):
        sq = (x * x).astype(BF16)
        ss = jnp.concatenate([_dot(sq[:, c * 256:(c + 1) * 256], gmat) for c in range(ATT_WIDTH // 256)], axis=1)
        xn = x * lax.rsqrt(ss * (1.0 / ATT_QK_DIM) + EPS) * w
        half = ROPE_DIM // 2
        slabs = [xn[:, c * LANES:(c + 1) * LANES] for c in range(rep)]
        up = jnp.concatenate([pltpu.roll(s, LANES - half, 1) for s in slabs], axis=1)
        dn = jnp.concatenate([pltpu.roll(s, half, 1) for s in slabs], axis=1)
        return xn * cos + up * sa + dn * sb

    q = prep(q_ref[...].astype(F32), qw_ref[...]) * (ATT_QK_DIM ** -0.5 * math.log2(math.e))
    qo_ref[...] = q.astype(qo_ref.dtype)
    k = prep(k_ref[...].astype(F32), kw_ref[...])
    kto_ref[0] = jnp.transpose(k).astype(kto_ref.dtype)


def _qk_prep(proj, qw, kw, cos_t, sa_t, sb_t, gmat, bsz, s_len, tm):
    t = proj.shape[0]
    nblk = s_len // tm
    full = lambda shape: pl.BlockSpec(shape, lambda b, c: (0,) * len(shape))
    return pl.pallas_call(
        _qkprep_kernel,
        grid=(bsz, nblk),
        in_specs=[
            pl.BlockSpec((tm, ATT_WIDTH), lambda b, c: (b * nblk + c, 1)),
            pl.BlockSpec((tm, ATT_WIDTH), lambda b, c: (b * nblk + c, 2)),
            full((1, ATT_WIDTH)), full((1, ATT_WIDTH)),
            pl.BlockSpec((tm, LANES), lambda b, c: (c, 0)),
            pl.BlockSpec((tm, LANES), lambda b, c: (c, 0)),
            pl.BlockSpec((tm, LANES), lambda b, c: (c, 0)),
            full((256, 256)),
        ],
        out_specs=[
            pl.BlockSpec((tm, ATT_WIDTH), lambda b, c: (b * nblk + c, 0)),
            pl.BlockSpec((1, ATT_WIDTH, tm), lambda b, c: (b, 0, c)),
        ],
        out_shape=[jax.ShapeDtypeStruct((t, ATT_WIDTH), BF16),
                   jax.ShapeDtypeStruct((bsz, ATT_WIDTH, s_len), BF16)],
        compiler_params=_cparams(("parallel", "parallel")),
        name="qk_prep",
    )(proj, proj, qw, kw, cos_t, sa_t, sb_t, gmat)


def _attn_kernel(q_ref, kt_ref, v_ref, lq1_ref, lk1_ref, lq2_ref, lk2_ref, sw_ref, o_ref,
                 m_ref, l_ref, acc_ref, *, tq, hp, lambda_init):
    i = pl.program_id(2)
    w = ATT_V_DIM
    lane = lax.broadcasted_iota(jnp.int32, (tq, w), 1)
    qqs = []
    for h in range(hp):
        q = q_ref[:, h * w:(h + 1) * w]
        zero = jnp.zeros_like(q)
        qqs.append(jnp.concatenate([jnp.where(lane < ATT_QK_DIM, q, zero),
                                    jnp.where(lane >= ATT_QK_DIM, q, zero)], axis=0))

    m_ref[...] = jnp.full(m_ref.shape, -jnp.inf, F32)
    l_ref[...] = jnp.zeros(l_ref.shape, F32)
    acc_ref[...] = jnp.zeros(acc_ref.shape, F32)

    def tile(j, mask):
        c0 = pl.multiple_of(j * tq, tq)
        for h in range(hp):
            s = _dot(qqs[h], kt_ref[0, h * w:(h + 1) * w, pl.ds(c0, tq)])
            if mask is not None:
                s = jnp.where(mask, s, -jnp.inf)
            m_prev = m_ref[h]
            m_new = jnp.maximum(m_prev, jnp.max(s, axis=1, keepdims=True))
            alpha = jnp.exp2(m_prev - m_new)
            p = jnp.exp2(s - jnp.concatenate([m_new] * (tq // LANES), axis=1))
            psum = p[:, 0:LANES]
            for c in range(1, tq // LANES):
                psum = psum + p[:, c * LANES:(c + 1) * LANES]
            l_ref[h] = alpha * l_ref[h] + psum
            acc_ref[h] = alpha * acc_ref[h] + _dot(p.astype(BF16), v_ref[pl.ds(c0, tq), h * w:(h + 1) * w])
            m_ref[h] = m_new

    def body(j, carry):
        tile(j, None)
        return carry

    lax.fori_loop(0, i, body, 0)

    row = lax.broadcasted_iota(jnp.int32, (2 * tq, tq), 0)
    col = lax.broadcasted_iota(jnp.int32, (2 * tq, tq), 1)
    row = jnp.where(row >= tq, row - tq, row)
    tile(i, (col // CHUNK) <= (row // CHUNK))

    lam = (jnp.exp(jnp.sum(lq1_ref[...] * lk1_ref[...], axis=1, keepdims=True))
           - jnp.exp(jnp.sum(lq2_ref[...] * lk2_ref[...], axis=1, keepdims=True)) + lambda_init)
    for h in range(hp):
        acc = acc_ref[h]
        l = jnp.sum(l_ref[h], axis=1, keepdims=True)
        o = acc[0:tq] / l[0:tq] - lam * (acc[tq:2 * tq] / l[tq:2 * tq])
        ms = jnp.mean(o * o, axis=-1, keepdims=True)
        o_ref[:, h * w:(h + 1) * w] = (o * lax.rsqrt(ms + EPS) * sw_ref[...] * (1.0 - lambda_init)).astype(o_ref.dtype)


def _attention(q_prep, k_t, proj, lq1, lk1, lq2, lk2, subln_w, lambda_init, bsz, s_len, tq, hp):
    t = q_prep.shape[0]
    nq = s_len // tq
    wblk = hp * ATT_V_DIM
    v_blk0 = (3 * D_MODEL) // wblk
    vec = lambda n: pl.BlockSpec((1, n), lambda b, h, i: (0, 0))
    return pl.pallas_call(
        functools.partial(_attn_kernel, tq=tq, hp=hp, lambda_init=lambda_init),
        grid=(bsz, ATT_HEADS // hp, nq),
        in_specs=[
            pl.BlockSpec((tq, wblk), lambda b, h, i: (b * nq + i, h)),
            pl.BlockSpec((1, wblk, s_len), lambda b, h, i: (b, h, 0)),
            pl.BlockSpec((s_len, wblk), lambda b, h, i: (b, v_blk0 + h)),
            vec(ATT_QK_DIM), vec(ATT_QK_DIM), vec(ATT_QK_DIM), vec(ATT_QK_DIM),
            vec(ATT_V_DIM),
        ],
        out_specs=pl.BlockSpec((tq, wblk), lambda b, h, i: (b * nq + i, h)),
        out_shape=jax.ShapeDtypeStruct((t, ATT_WIDTH), BF16),
        scratch_shapes=[
            pltpu.VMEM((hp, 2 * tq, LANES), F32),
            pltpu.VMEM((hp, 2 * tq, LANES), F32),
            pltpu.VMEM((hp, 2 * tq, ATT_V_DIM), F32),
        ],
        compiler_params=_cparams(("parallel", "parallel", "arbitrary")),
        name="attn",
    )(q_prep, k_t, proj, lq1, lk1, lq2, lk2, subln_w)


def _merge_kernel(ys_ref, ya_ref, gs_ref, ga_ref, x_ref, gb_ref, ws_ref, wa_ref, wo_ref, nw_ref, *rest,
                  with_router):
    if with_router:
        rw_ref, tril_ref, xo_ref, hn3_ref, w12_ref, pos_ref, cnt_ref, run_ref = rest
    else:
        xo_ref, hn_ref = rest
    gs = _sigmoid(gs_ref[...].astype(F32) + gb_ref[0:1, :])
    ga = _sigmoid(ga_ref[...].astype(F32) + gb_ref[1:2, :])
    merged = gs * _dot(ys_ref[...], ws_ref[...]) + ga * _dot(ya_ref[...], wa_ref[...])
    xn = x_ref[...] + _dot(merged.astype(BF16), wo_ref[...])
    xo_ref[...] = xn
    ms = jnp.mean(xn * xn, axis=-1, keepdims=True)
    hn = xn * lax.rsqrt(ms + EPS) * nw_ref[...]
    if not with_router:
        hn_ref[...] = hn.astype(hn_ref.dtype)
        return

    @pl.when(pl.program_id(0) == 0)
    def _():
        run_ref[...] = jnp.zeros_like(run_ref)

    for s in range(D_MODEL // LANES):
        hn3_ref[:, s, :] = hn[:, s * LANES:(s + 1) * LANES]

    h1, h2, _ = _split3(hn)
    logits = _dot(h1, rw_ref[0]) + _dot(h1, rw_ref[1]) + _dot(h2, rw_ref[0])
    lane = lax.broadcasted_iota(jnp.int32, logits.shape, 1)
    neg = jnp.float32(-jnp.inf)
    lg = jnp.where(lane < N_EXPERTS, logits, neg)
    m1 = jnp.max(lg, axis=1, keepdims=True)
    i1 = jnp.min(jnp.where(lg == m1, lane, LANES), axis=1, keepdims=True)
    lg2 = jnp.where(lane == i1, neg, lg)
    m2 = jnp.max(lg2, axis=1, keepdims=True)
    i2 = jnp.min(jnp.where(lg2 == m2, lane, LANES), axis=1, keepdims=True)
    e2 = jnp.exp(m2 - m1)
    w1 = 1.0 / (1.0 + e2)
    w2 = e2 / (1.0 + e2)
    w12_ref[...] = jnp.where(lane == 0, w1, 0.0) + jnp.where(lane == 1, w2, 0.0)

    sel = jnp.where(lane == i1, 1.0, 0.0) + jnp.where(lane == i2, 1.0, 0.0)
    rank = run_ref[...] + _dot(tril_ref[...], sel.astype(BF16))
    r1 = jnp.sum(jnp.where(lane == i1, rank, 0.0), axis=1, keepdims=True)
    r2 = jnp.sum(jnp.where(lane == i2, rank, 0.0), axis=1, keepdims=True)
    route = (jnp.where(lane == 0, i1.astype(F32), 0.0) + jnp.where(lane == 1, i2.astype(F32), 0.0)
             + jnp.where(lane == 2, r1, 0.0) + jnp.where(lane == 3, r2, 0.0))
    pos_ref[...] = jnp.transpose(route)[0:8, :].astype(jnp.int32)
    run_ref[...] = run_ref[...] + jnp.sum(sel, axis=0, keepdims=True)
    cnt_ref[...] = run_ref[...]


def _merge(y_ssd, y_att, proj, x2d, gate_b, ws, wa, wo, nw, router_w, tril, tm):
    t = x2d.shape[0]
    with_router = router_w is not None
    row = lambda w, blk: pl.BlockSpec((tm, w), lambda i: (i, blk))
    full = lambda shape: pl.BlockSpec(shape, lambda i: (0,) * len(shape))
    in_specs = [
        row(D_MODEL, 0), row(D_MODEL, 0), row(D_MODEL, 4), row(D_MODEL, 5), row(D_MODEL, 0),
        full((2, D_MODEL)), full((D_MODEL, D_MODEL)), full((D_MODEL, D_MODEL)), full((D_MODEL, D_MODEL)),
        full((1, D_MODEL)),
    ]
    args = [y_ssd, y_att, proj, proj, x2d, gate_b, ws, wa, wo, nw]
    scratch = []
    if with_router:
        in_specs += [full((2, D_MODEL, LANES)), full((tm, tm))]
        args += [router_w, tril]
        out_specs = [
            row(D_MODEL, 0),
            pl.BlockSpec((tm, D_MODEL // LANES, LANES), lambda i: (i, 0, 0)),
            row(LANES, 0),
            pl.BlockSpec((8, tm), lambda i: (0, i)),
            full((1, LANES)),
        ]
        out_shape = [
            jax.ShapeDtypeStruct((t, D_MODEL), F32),
            jax.ShapeDtypeStruct((t, D_MODEL // LANES, LANES), F32),
            jax.ShapeDtypeStruct((t, LANES), F32),
            jax.ShapeDtypeStruct((8, t), jnp.int32),
            jax.ShapeDtypeStruct((1, LANES), F32),
        ]
        scratch = [pltpu.VMEM((1, LANES), F32)]
    else:
        out_specs = [row(D_MODEL, 0), row(D_MODEL, 0)]
        out_shape = [jax.ShapeDtypeStruct((t, D_MODEL), F32), jax.ShapeDtypeStruct((t, D_MODEL), BF16)]
    return pl.pallas_call(
        functools.partial(_merge_kernel, with_router=with_router),
        grid=(t // tm,),
        in_specs=in_specs,
        out_specs=out_specs,
        out_shape=out_shape,
        scratch_shapes=scratch,
        compiler_params=_cparams(("arbitrary",) if with_router else ("parallel",)),
        name="merge_router" if with_router else "merge",
    )(*args)


def _ffn_kernel(hn_ref, x_ref, wg_ref, wu_ref, wd_ref, o_ref):
    f = pl.program_id(1)
    hn = hn_ref[...]
    h = (_silu(_dot(hn, wg_ref[...])) * _dot(hn, wu_ref[...])).astype(BF16)
    y = _dot(h, wd_ref[...])

    @pl.when(f == 0)
    def _():
        o_ref[...] = x_ref[...] + y

    @pl.when(f > 0)
    def _():
        o_ref[...] += y


def _ffn(hn, x2d, wg, wu, wd, tm, tf):
    t = x2d.shape[0]
    return pl.pallas_call(
        _ffn_kernel,
        grid=(t // tm, D_FF // tf),
        in_specs=[
            pl.BlockSpec((tm, D_MODEL), lambda i, f: (i, 0)),
            pl.BlockSpec((tm, D_MODEL), lambda i, f: (i, 0)),
            pl.BlockSpec((D_MODEL, tf), lambda i, f: (0, f)),
            pl.BlockSpec((D_MODEL, tf), lambda i, f: (0, f)),
            pl.BlockSpec((tf, D_MODEL), lambda i, f: (f, 0)),
        ],
        out_specs=pl.BlockSpec((tm, D_MODEL), lambda i, f: (i, 0)),
        out_shape=jax.ShapeDtypeStruct((t, D_MODEL), F32),
        compiler_params=_cparams(("parallel", "arbitrary")),
        name="ffn",
    )(hn, x2d, wg, wu, wd)


META_OFF, META_LAST, META_NVALID = 0, N_EXPERTS, 2 * N_EXPERTS


def _dispatch_kernel(route_ref, meta_ref, hn3_ref, zeros_ref, xs3_ref, sem, zsem, *, tm, n_tiles):
    i = pl.program_id(0)

    @pl.when(i == 0)
    def _():
        def ztile(row0):
            return pltpu.make_async_copy(zeros_ref, xs3_ref.at[pl.ds(pl.multiple_of(row0, MOE_TILE), MOE_TILE)], zsem)

        def each_zero_tile(fn):
            for e in range(N_EXPERTS):
                last = meta_ref[META_LAST + e]

                @pl.when(last >= 0)
                def _():
                    fn(ztile(last))
            for k in range(N_EXPERTS):
                g = meta_ref[META_NVALID] + k

                @pl.when(g < n_tiles)
                def _():
                    fn(ztile(g * MOE_TILE))

        each_zero_tile(lambda c: c.start())
        each_zero_tile(lambda c: c.wait())

    def body(t, carry):
        src = hn3_ref.at[i * tm + t]
        d1 = meta_ref[META_OFF + route_ref[0, t]] + route_ref[2, t]
        d2 = meta_ref[META_OFF + route_ref[1, t]] + route_ref[3, t]
        pltpu.make_async_copy(src, xs3_ref.at[d1], sem).start()
        pltpu.make_async_copy(src, xs3_ref.at[d2], sem).start()
        return carry

    lax.fori_loop(0, tm, body, 0)
    for _ in range(2):
        pltpu.make_async_copy(hn3_ref.at[pl.ds(0, tm)], xs3_ref.at[pl.ds(0, tm)], sem).wait()


def _dispatch(route, meta, hn3, zeros_tile, tm, n_tiles):
    t = hn3.shape[0]
    return pl.pallas_call(
        functools.partial(_dispatch_kernel, tm=tm, n_tiles=n_tiles),
        grid=(t // tm,),
        in_specs=[
            pl.BlockSpec((8, tm), lambda i: (0, i), memory_space=pltpu.SMEM),
            pl.BlockSpec(memory_space=pltpu.SMEM),
            pl.BlockSpec(memory_space=pl.ANY),
            pl.BlockSpec(memory_space=pl.ANY),
        ],
        out_specs=pl.BlockSpec(memory_space=pl.ANY),
        out_shape=jax.ShapeDtypeStruct((n_tiles * MOE_TILE,) + hn3.shape[1:], F32),
        scratch_shapes=[pltpu.SemaphoreType.DMA, pltpu.SemaphoreType.DMA],
        compiler_params=_cparams(("arbitrary",)),
        name="moe_dispatch",
    )(route, meta, hn3, zeros_tile)


def _gffn_kernel(te_ref, nv_ref, x3_ref, wg_ref, wu_ref, wd_ref, y3_ref, xb_ref, acc_ref):
    g = pl.program_id(0)
    f = pl.program_id(1)
    nsl = D_MODEL // LANES

    @pl.when((g >= nv_ref[0]) & (f == 0))
    def _():
        y3_ref[...] = jnp.zeros_like(y3_ref)

    @pl.when(g < nv_ref[0])
    def _():
        @pl.when(f == 0)
        def _():
            xb_ref[...] = jnp.concatenate([x3_ref[:, s, :] for s in range(nsl)], axis=1).astype(BF16)

        xb = xb_ref[...]
        h = (_silu(_dot(xb, wg_ref[0])) * _dot(xb, wu_ref[0])).astype(BF16)
        y = _dot(h, wd_ref[0])

        @pl.when(f == 0)
        def _():
            acc_ref[...] = y

        @pl.when(f == pl.num_programs(1) - 1)
        def _():
            tot = acc_ref[...] + y
            for s in range(nsl):
                y3_ref[:, s, :] = tot[:, s * LANES:(s + 1) * LANES]


def _grouped_ffn(tile_expert, n_valid, xs3, wg, wu, wd, tf):
    n_tiles = tile_expert.shape[0]
    nsl = D_MODEL // LANES
    assert D_FF // tf == 2
    grid_spec = pltpu.PrefetchScalarGridSpec(
        num_scalar_prefetch=2,
        grid=(n_tiles, D_FF // tf),
        in_specs=[
            pl.BlockSpec((MOE_TILE, nsl, LANES), lambda g, f, te, nv: (g, 0, 0)),
            pl.BlockSpec((1, D_MODEL, tf), lambda g, f, te, nv: (te[g], 0, f)),
            pl.BlockSpec((1, D_MODEL, tf), lambda g, f, te, nv: (te[g], 0, f)),
            pl.BlockSpec((1, tf, D_MODEL), lambda g, f, te, nv: (te[g], f, 0)),
        ],
        out_specs=pl.BlockSpec((MOE_TILE, nsl, LANES), lambda g, f, te, nv: (g, 0, 0)),
        scratch_shapes=[pltpu.VMEM((MOE_TILE, D_MODEL), BF16), pltpu.VMEM((MOE_TILE, D_MODEL), F32)],
    )
    return pl.pallas_call(
        _gffn_kernel,
        grid_spec=grid_spec,
        out_shape=jax.ShapeDtypeStruct(xs3.shape, F32),
        compiler_params=_cparams(("arbitrary", "arbitrary")),
        name="moe_ffn",
    )(tile_expert, n_valid, xs3, wg, wu, wd)


def _combine_kernel(route_ref, meta_ref, w12_ref, x_ref, y3_ref, o_ref, buf_ref, sem, *, tm):
    def body(t, carry):
        d1 = meta_ref[META_OFF + route_ref[0, t]] + route_ref[2, t]
        d2 = meta_ref[META_OFF + route_ref[1, t]] + route_ref[3, t]
        pltpu.make_async_copy(y3_ref.at[d1], buf_ref.at[0, t], sem).start()
        pltpu.make_async_copy(y3_ref.at[d2], buf_ref.at[1, t], sem).start()
        return carry

    lax.fori_loop(0, tm, body, 0)
    for k in range(2):
        pltpu.make_async_copy(y3_ref.at[pl.ds(0, tm)], buf_ref.at[k], sem).wait()
    w = w12_ref[...]
    nsl = D_MODEL // LANES
    y1 = jnp.concatenate([buf_ref[0, :, s, :] for s in range(nsl)], axis=1)
    y2 = jnp.concatenate([buf_ref[1, :, s, :] for s in range(nsl)], axis=1)
    o_ref[...] = x_ref[...] + w[:, 0:1] * y1 + w[:, 1:2] * y2


def _combine(route, meta, w12, x2d, y3, tm):
    t = x2d.shape[0]
    nsl = D_MODEL // LANES
    return pl.pallas_call(
        functools.partial(_combine_kernel, tm=tm),
        grid=(t // tm,),
        in_specs=[
            pl.BlockSpec((8, tm), lambda i: (0, i), memory_space=pltpu.SMEM),
            pl.BlockSpec(memory_space=pltpu.SMEM),
            pl.BlockSpec((tm, LANES), lambda i: (i, 0)),
            pl.BlockSpec((tm, D_MODEL), lambda i: (i, 0)),
            pl.BlockSpec(memory_space=pl.ANY),
        ],
        out_specs=pl.BlockSpec((tm, D_MODEL), lambda i: (i, 0)),
        out_shape=jax.ShapeDtypeStruct((t, D_MODEL), F32),
        scratch_shapes=[pltpu.VMEM((2, tm, nsl, LANES), F32), pltpu.SemaphoreType.DMA],
        compiler_params=_cparams(("arbitrary",)),
        name="moe_combine",
    )(route, meta, w12, x2d, y3)


def _moe_tiles(cnt_f, n_tiles):
    cnt = cnt_f[0, :N_EXPERTS].astype(jnp.int32)
    per = (cnt + (MOE_TILE - 1)) // MOE_TILE
    ends = jnp.cumsum(per)
    starts = ends - per
    n_valid = ends[-1:]
    g = jnp.minimum(jnp.arange(n_tiles, dtype=jnp.int32), n_valid - 1)
    tile_expert = jnp.sum((g[:, None] >= ends[None, :]).astype(jnp.int32), axis=1)
    last = jnp.where(per > 0, (ends - 1) * MOE_TILE, -1)
    meta = jnp.concatenate([starts * MOE_TILE, last, n_valid]).astype(jnp.int32)
    return tile_expert.astype(jnp.int32), n_valid.astype(jnp.int32), jnp.pad(meta, (0, LANES - meta.shape[0]))


def _ssd_consts():
    e_mat = np.zeros((LANES, SSD_INNER), np.float32)
    for h in range(SSD_HEADS):
        e_mat[h, h * SSD_HEAD_DIM:(h + 1) * SSD_HEAD_DIM] = 1.0
    tril = np.tril(np.ones((CHUNK, CHUNK), np.float32))
    dmask = np.tile(np.eye(CHUNK, dtype=np.float32), (1, SSD_HEADS))
    trilmask = np.tile(tril, (1, SSD_HEADS))
    blk = np.arange(256) // SSD_HEAD_DIM
    bdmask = (blk[:, None] == blk[None, :]).astype(np.float32)
    return (jnp.asarray(e_mat, BF16), jnp.asarray(tril, BF16), jnp.asarray(dmask, F32),
            jnp.asarray(trilmask, F32), jnp.asarray(bdmask, BF16))


def _rope_tables(s_len):
    half = ROPE_DIM // 2
    inv = ROPE_THETA ** (-jnp.arange(0, ROPE_DIM, 2, dtype=F32) / ROPE_DIM)
    ang = jnp.arange(s_len, dtype=F32)[:, None] * inv[None, :]
    cos, sin = jnp.cos(ang), jnp.sin(ang)
    pad = jnp.zeros((s_len, ATT_QK_DIM - ROPE_DIM), F32)
    cos_c = jnp.concatenate([cos, cos, pad + 1.0], axis=1)
    sa_c = jnp.concatenate([-sin, jnp.zeros_like(sin), pad], axis=1)
    sb_c = jnp.concatenate([jnp.zeros_like(sin), sin, pad], axis=1)
    two = lambda a: jnp.concatenate([a, a], axis=1)
    return two(cos_c), two(sa_c), two(sb_c)


def _split2_w(w):
    w = jnp.pad(w, ((0, 0), (0, LANES - w.shape[1])))
    hi = w.astype(BF16)
    lo = (w - hi.astype(F32)).astype(BF16)
    return jnp.stack([hi, lo])


def _pick(t, prefs):
    for p in prefs:
        if t % p == 0:
            return p
    return t


def kernel(x, norm_mix_w, w_in, conv_w, conv_b, dt_bias, a_log, d_skip, ssd_norm_w, q_norm_w, k_norm_w,
           lambda_q1, lambda_k1, lambda_q2, lambda_k2, subln_w, gate_b, w_br_ssd, w_br_att, w_out,
           norm_ffn_w, ffn_w_gate, ffn_w_up, ffn_w_down, router_w, moe_w_gate, moe_w_up, moe_w_down):
    bsz, s_len, d = x.shape
    depth = w_in.shape[0]
    t = bsz * s_len
    assert d == D_MODEL and s_len % 512 == 0

    tm_proj = _pick(t, (1024, 512))
    tm_row = _pick(t, (512,))
    lb = 256
    tm_qk = 512
    tq, heads_per_step = 512, 2

    consts = _ssd_consts()
    tril_tok = jnp.asarray(np.tril(np.ones((tm_row, tm_row), np.float32), -1), BF16)
    zeros_tile = jnp.zeros((MOE_TILE, D_MODEL // LANES, LANES), F32)
    cos_t, sa_t, sb_t = _rope_tables(s_len)
    blk64 = np.arange(256) // ATT_QK_DIM
    gmat = jnp.asarray((blk64[:, None] == blk64[None, :]).astype(np.float32), BF16)

    o1 = SSD_INNER
    o2 = o1 + SSD_CONV_DIM
    o3 = o2 + SSD_HEADS
    o4 = o3 + ATT_WIDTH
    o5 = o4 + ATT_WIDTH
    o6 = o5 + ATT_WIDTH
    o7 = o6 + D_MODEL

    x2d = x.reshape(t, d)
    for i in range(depth):
        lambda_init = 0.8 - 0.6 * math.exp(-0.3 * i)
        wi = w_in[i]
        w_main = jnp.concatenate([wi[:, :o1], wi[:, o3:], wi[:, o1:o2]], axis=1).astype(BF16)
        w_dt = _split2_w(wi[:, o2:o3])
        proj, dt_raw = _in_proj(x2d, norm_mix_w[i][None, :], w_main, w_dt, tm_proj, SSD_CONV_DIM)

        dtb = jnp.pad(dt_bias[i], (0, LANES - SSD_HEADS))[None, :]
        aexp = jnp.repeat(-jnp.exp(a_log[i].astype(F32)), SSD_HEAD_DIM)[None, :]
        dexp = jnp.repeat(d_skip[i], SSD_HEAD_DIM)[None, :]
        y_ssd = _ssd(proj, dt_raw, conv_w[i], conv_b[i][None, :], dtb, aexp, dexp, ssd_norm_w[i][None, :],
                     consts, bsz, s_len, lb)

        qw = jnp.tile(q_norm_w[i], ATT_WIDTH // ATT_QK_DIM)[None, :]
        kw = jnp.tile(k_norm_w[i], ATT_WIDTH // ATT_QK_DIM)[None, :]
        q_prep, k_t = _qk_prep(proj, qw, kw, cos_t, sa_t, sb_t, gmat, bsz, s_len, tm_qk)
        y_att = _attention(q_prep, k_t, proj, lambda_q1[i][None, :], lambda_k1[i][None, :],
                           lambda_q2[i][None, :], lambda_k2[i][None, :], subln_w[i][None, :],
                           lambda_init, bsz, s_len, tq, heads_per_step)

        j = i // 2
        is_moe = i % 2 == 1
        rw = _split2_w(router_w[j]) if is_moe else None
        outs = _merge(y_ssd, y_att, proj, x2d, gate_b[i], w_br_ssd[i].astype(BF16), w_br_att[i].astype(BF16),
                      w_out[i].astype(BF16), norm_ffn_w[i][None, :], rw, tril_tok, tm_row)
        if is_moe:
            x2d, hn3, w12, route, cnt_f = outs
            n_tiles = (2 * t) // MOE_TILE + N_EXPERTS
            tile_e, n_valid, meta = _moe_tiles(cnt_f, n_tiles)
            xs3 = _dispatch(route, meta, hn3, zeros_tile, 1024, n_tiles)
            y3 = _grouped_ffn(tile_e, n_valid, xs3, moe_w_gate[j].astype(BF16),
                              moe_w_up[j].astype(BF16), moe_w_down[j].astype(BF16), D_FF // 2)
            x2d = _combine(route, meta, w12, x2d, y3, tm_row)
        else:
            x2d, hn = outs
            x2d = _ffn(hn, x2d, ffn_w_gate[j].astype(BF16), ffn_w_up[j].astype(BF16),
                       ffn_w_down[j].astype(BF16), tm_row, D_FF // 2)
    return x2d.reshape(bsz, s_len, d)
```

```python
import functools
import math

import numpy as np
import jax
import jax.numpy as jnp
from jax import lax
from jax.experimental import pallas as pl
from jax.experimental.pallas import tpu as pltpu

F32 = jnp.float32
BF16 = jnp.bfloat16

D_MODEL = 1024
CHUNK = 64
EPS = 1e-6
SSD_HEADS = 16
SSD_HEAD_DIM = 64
SSD_INNER = SSD_HEADS * SSD_HEAD_DIM
SSD_GROUPS = 2
SSD_STATE = 128
SSD_CONV = 4
SSD_CONV_DIM = SSD_INNER + 2 * SSD_GROUPS * SSD_STATE
ATT_HEADS = 8
ATT_QK_DIM = 64
ATT_V_DIM = 128
ATT_WIDTH = ATT_HEADS * ATT_V_DIM
ROPE_THETA = 500000.0
ROPE_DIM = ATT_QK_DIM // 4
D_FF = 2816
N_EXPERTS = 8
MOE_TILE = 512
LANES = 128
GROUP_W = SSD_INNER // SSD_GROUPS
MAIN_COLS = 6 * D_MODEL + SSD_CONV_DIM
VMEM_LIMIT = 56 * 1024 * 1024


def _cparams(sem):
    return pltpu.CompilerParams(dimension_semantics=sem, vmem_limit_bytes=VMEM_LIMIT)


def _split3(a):
    h1 = a.astype(BF16)
    r1 = a - h1.astype(F32)
    h2 = r1.astype(BF16)
    r2 = r1 - h2.astype(F32)
    return h1, h2, r2.astype(BF16)


def _dot(a, b):
    return jnp.dot(a, b, preferred_element_type=F32)


def _sigmoid(x):
    return 1.0 / (1.0 + jnp.exp(-x))


def _silu(x):
    return x * _sigmoid(x)


def _inproj_kernel(x_ref, nw_ref, w_ref, wdt_ref, o_ref, dt_ref, xn_ref):
    @pl.when(pl.program_id(1) == 0)
    def _():
        x = x_ref[...]
        ms = jnp.mean(x * x, axis=-1, keepdims=True)
        xn = x * lax.rsqrt(ms + EPS) * nw_ref[...]
        x1, x2, _ = _split3(xn)
        xn_ref[...] = x1
        dt_ref[...] = _dot(x1, wdt_ref[0]) + _dot(x1, wdt_ref[1]) + _dot(x2, wdt_ref[0])

    o_ref[...] = _dot(xn_ref[...], w_ref[...]).astype(o_ref.dtype)


def _in_proj(x2d, norm_w, w_main, w_dt, tm, tn):
    t, d = x2d.shape
    n = w_main.shape[1]
    return pl.pallas_call(
        _inproj_kernel,
        grid=(t // tm, n // tn),
        in_specs=[
            pl.BlockSpec((tm, d), lambda i, j: (i, 0)),
            pl.BlockSpec((1, d), lambda i, j: (0, 0)),
            pl.BlockSpec((d, tn), lambda i, j: (0, j)),
            pl.BlockSpec((2, d, LANES), lambda i, j: (0, 0, 0)),
        ],
        out_specs=[
            pl.BlockSpec((tm, tn), lambda i, j: (i, j)),
            pl.BlockSpec((tm, LANES), lambda i, j: (i, 0)),
        ],
        out_shape=[jax.ShapeDtypeStruct((t, n), BF16), jax.ShapeDtypeStruct((t, LANES), F32)],
        scratch_shapes=[pltpu.VMEM((tm, d), BF16)],
        compiler_params=_cparams(("parallel", "arbitrary")),
        name="in_proj",
    )(x2d, norm_w, w_main, w_dt)


def _ssd_kernel(xbc_ref, z_ref, dt_ref, convw_ref, convb_ref, dtb_ref, aexp_ref, dexp_ref, nw_ref,
                e_ref, tril_ref, dmask_ref, trilmask_ref, bdmask_ref,
                o_ref, xpad_ref, u_ref, dte_ref, state_ref, *, lb):
    @pl.when(pl.program_id(1) == 0)
    def _():
        xpad_ref[0:8, :] = jnp.zeros((8, SSD_CONV_DIM), F32)
        state_ref[...] = jnp.zeros_like(state_ref)

    xpad_ref[8:8 + lb, :] = xbc_ref[...].astype(F32)
    acc = convb_ref[...] + convw_ref[0:1, :] * xpad_ref[pl.ds(8 - (SSD_CONV - 1), lb), :]
    for k in range(1, SSD_CONV):
        acc = acc + convw_ref[k:k + 1, :] * xpad_ref[pl.ds(8 - (SSD_CONV - 1) + k, lb), :]
    u_ref[...] = _silu(acc)
    xpad_ref[0:8, :] = xpad_ref[lb:lb + 8, :]

    t = dt_ref[...] + dtb_ref[...]
    dt = jnp.maximum(t, 0.0) + jnp.log(1.0 + jnp.exp(-jnp.abs(t)))
    d1, d2, d3 = _split3(dt)
    e = e_ref[...]
    dte_ref[...] = _dot(d1, e) + _dot(d2, e) + _dot(d3, e)

    tril = tril_ref[...]
    dmask = dmask_ref[...]
    trilmask = trilmask_ref[...] > 0.5
    bdmask = bdmask_ref[...]
    aexp = aexp_ref[...]
    dexp = dexp_ref[...]
    nw = nw_ref[...]

    def chunk(j, carry):
        r0 = pl.multiple_of(j * CHUNK, CHUNK)
        rows = pl.ds(r0, CHUNK)
        xs = u_ref[rows, 0:SSD_INNER]
        bm = u_ref[rows, SSD_INNER:SSD_INNER + SSD_GROUPS * SSD_STATE]
        cm = u_ref[rows, SSD_INNER + SSD_GROUPS * SSD_STATE:SSD_CONV_DIM]
        dte = dte_ref[rows, :]
        a1, a2, a3 = _split3(dte * aexp)
        cs = _dot(tril, a1) + _dot(tril, a2) + _dot(tril, a3)
        rowv = jnp.sum(cs * dmask, axis=0, keepdims=True)
        decay = jnp.exp(jnp.where(trilmask, cs - rowv, -1e30))
        a_end = cs[CHUNK - 1:CHUNK, :]
        to_end = jnp.exp(a_end - cs)
        xdt = xs * dte
        xdt_b = xdt.astype(BF16)
        xw_b = (xdt * to_end).astype(BF16)
        bm_b = bm.astype(BF16)
        cm_b = cm.astype(BF16)

        cb_parts = []
        yoff_parts = []
        for g in range(SSD_GROUPS):
            bg = bm_b[:, g * SSD_STATE:(g + 1) * SSD_STATE]
            cg = cm_b[:, g * SSD_STATE:(g + 1) * SSD_STATE]
            cb = lax.dot_general(cg, bg, (((1,), (1,)), ((), ())), preferred_element_type=F32)
            cb2 = jnp.concatenate([cb, cb], axis=1)
            cb_parts += [cb2] * (GROUP_W // LANES)
            st = state_ref[g]
            yoff_parts.append(_dot(cg, st.astype(BF16)))
            bg_t = jnp.transpose(bm[:, g * SSD_STATE:(g + 1) * SSD_STATE]).astype(BF16)
            upd = _dot(bg_t, xw_b[:, g * GROUP_W:(g + 1) * GROUP_W])
            state_ref[g] = jnp.exp(a_end[:, g * GROUP_W:(g + 1) * GROUP_W]) * st + upd
        m_b = (jnp.concatenate(cb_parts, axis=1) * decay).astype(BF16)
        y_off = jnp.concatenate(yoff_parts, axis=1) * jnp.exp(cs)

        yd_parts = []
        for q in range(SSD_INNER // 256):
            sl = slice(q * 256, (q + 1) * 256)
            rhs = jnp.concatenate([xdt_b[:, sl]] * 4, axis=0) * bdmask
            yd_parts.append(_dot(m_b[:, sl], rhs))
        y = jnp.concatenate(yd_parts, axis=1) + y_off + xs * dexp

        yz = y * _silu(z_ref[rows, :].astype(F32))
        outs = []
        for g in range(SSD_GROUPS):
            yg = yz[:, g * GROUP_W:(g + 1) * GROUP_W]
            ms = jnp.mean(yg * yg, axis=-1, keepdims=True)
            outs.append(yg * lax.rsqrt(ms + EPS))
        o_ref[rows, :] = (jnp.concatenate(outs, axis=1) * nw).astype(o_ref.dtype)
        return carry

    lax.fori_loop(0, lb // CHUNK, chunk, 0)


def _ssd(proj, dt_raw, conv_w, conv_b, dtb, aexp, dexp, norm_w, consts, bsz, s_len, lb):
    t = proj.shape[0]
    nblk = s_len // lb
    xbc_blk0 = (6 * D_MODEL) // SSD_CONV_DIM
    full = lambda shape: pl.BlockSpec(shape, lambda b, c: (0,) * len(shape))
    e_mat, tril, dmask, trilmask, bdmask = consts
    return pl.pallas_call(
        functools.partial(_ssd_kernel, lb=lb),
        grid=(bsz, nblk),
        in_specs=[
            pl.BlockSpec((lb, SSD_CONV_DIM), lambda b, c: (b * nblk + c, xbc_blk0)),
            pl.BlockSpec((lb, SSD_INNER), lambda b, c: (b * nblk + c, 0)),
            pl.BlockSpec((lb, LANES), lambda b, c: (b * nblk + c, 0)),
            full((SSD_CONV, SSD_CONV_DIM)),
            full((1, SSD_CONV_DIM)),
            full((1, LANES)),
            full((1, SSD_INNER)),
            full((1, SSD_INNER)),
            full((1, SSD_INNER)),
            full(e_mat.shape), full(tril.shape), full(dmask.shape), full(trilmask.shape), full(bdmask.shape),
        ],
        out_specs=pl.BlockSpec((lb, SSD_INNER), lambda b, c: (b * nblk + c, 0)),
        out_shape=jax.ShapeDtypeStruct((t, SSD_INNER), BF16),
        scratch_shapes=[
            pltpu.VMEM((lb + 8, SSD_CONV_DIM), F32),
            pltpu.VMEM((lb, SSD_CONV_DIM), F32),
            pltpu.VMEM((lb, SSD_INNER), F32),
            pltpu.VMEM((SSD_GROUPS, SSD_STATE, GROUP_W), F32),
        ],
        compiler_params=_cparams(("parallel", "arbitrary")),
        name="ssd",
    )(proj, proj, dt_raw, conv_w, conv_b, dtb, aexp, dexp, norm_w, e_mat, tril, dmask, trilmask, bdmask)


def _qkprep_kernel(q_ref, k_ref, qw_ref, kw_ref, cos_ref, sa_ref, sb_ref, g_ref, qo_ref, kto_ref):
    rep = ATT_WIDTH // LANES
    cos = jnp.concatenate([cos_ref[...]] * rep, axis=1)
    sa = jnp.concatenate([sa_ref[...]] * rep, axis=1)
    sb = jnp.concatenate([sb_ref[...]] * rep, axis=1)
    gmat = g_ref[...]

    def prep(x, w):
        sq = (x * x).astype(BF16)
        ss = jnp.concatenate([_dot(sq[:, c * 256:(c + 1) * 256], gmat) for c in range(ATT_WIDTH // 256)], axis=1)
        xn = x * lax.rsqrt(ss * (1.0 / ATT_QK_DIM) + EPS) * w
        half = ROPE_DIM // 2
        slabs = [xn[:, c * LANES:(c + 1) * LANES] for c in range(rep)]
        up = jnp.concatenate([pltpu.roll(s, LANES - half, 1) for s in slabs], axis=1)
        dn = jnp.concatenate([pltpu.roll(s, half, 1) for s in slabs], axis=1)
        return xn * cos + up * sa + dn * sb

    q = prep(q_ref[...].astype(F32), qw_ref[...]) * (ATT_QK_DIM ** -0.5 * math.log2(math.e))
    qo_ref[...] = q.astype(qo_ref.dtype)
    k = prep(k_ref[...].astype(F32), kw_ref[...])
    kto_ref[0] = jnp.transpose(k).astype(kto_ref.dtype)


def _qk_prep(proj, qw, kw, cos_t, sa_t, sb_t, gmat, bsz, s_len, tm):
    t = proj.shape[0]
    nblk = s_len // tm
    full = lambda shape: pl.BlockSpec(shape, lambda b, c: (0,) * len(shape))
    return pl.pallas_call(
        _qkprep_kernel,
        grid=(bsz, nblk),
        in_specs=[
            pl.BlockSpec((tm, ATT_WIDTH), lambda b, c: (b * nblk + c, 1)),
            pl.BlockSpec((tm, ATT_WIDTH), lambda b, c: (b * nblk + c, 2)),
            full((1, ATT_WIDTH)), full((1, ATT_WIDTH)),
            pl.BlockSpec((tm, LANES), lambda b, c: (c, 0)),
            pl.BlockSpec((tm, LANES), lambda b, c: (c, 0)),
            pl.BlockSpec((tm, LANES), lambda b, c: (c, 0)),
            full((256, 256)),
        ],
        out_specs=[
            pl.BlockSpec((tm, ATT_WIDTH), lambda b, c: (b * nblk + c, 0)),
            pl.BlockSpec((1, ATT_WIDTH, tm), lambda b, c: (b, 0, c)),
        ],
        out_shape=[jax.ShapeDtypeStruct((t, ATT_WIDTH), BF16),
                   jax.ShapeDtypeStruct((bsz, ATT_WIDTH, s_len), BF16)],
        compiler_params=_cparams(("parallel", "parallel")),
        name="qk_prep",
    )(proj, proj, qw, kw, cos_t, sa_t, sb_t, gmat)


def _attn_kernel(q_ref, kt_ref, v_ref, lq1_ref, lk1_ref, lq2_ref, lk2_ref, sw_ref, o_ref,
                 m_ref, l_ref, acc_ref, *, tq, hp, lambda_init):
    i = pl.program_id(2)
    w = ATT_V_DIM
    lane = lax.broadcasted_iota(jnp.int32, (tq, w), 1)
    qqs = []
    for h in range(hp):
        q = q_ref[:, h * w:(h + 1) * w]
        zero = jnp.zeros_like(q)
        qqs.append(jnp.concatenate([jnp.where(lane < ATT_QK_DIM, q, zero),
                                    jnp.where(lane >= ATT_QK_DIM, q, zero)], axis=0))

    m_ref[...] = jnp.full(m_ref.shape, -jnp.inf, F32)
    l_ref[...] = jnp.zeros(l_ref.shape, F32)
    acc_ref[...] = jnp.zeros(acc_ref.shape, F32)

    def tile(j, mask):
        c0 = pl.multiple_of(j * tq, tq)
        for h in range(hp):
            s = _dot(qqs[h], kt_ref[0, h * w:(h + 1) * w, pl.ds(c0, tq)])
            if mask is not None:
                s = jnp.where(mask, s, -jnp.inf)
            m_prev = m_ref[h]
            m_new = jnp.maximum(m_prev, jnp.max(s, axis=1, keepdims=True))
            alpha = jnp.exp2(m_prev - m_new)
            p = jnp.exp2(s - jnp.concatenate([m_new] * (tq // LANES), axis=1))
            psum = p[:, 0:LANES]
            for c in range(1, tq // LANES):
                psum = psum + p[:, c * LANES:(c + 1) * LANES]
            l_ref[h] = alpha * l_ref[h] + psum
            acc_ref[h] = alpha * acc_ref[h] + _dot(p.astype(BF16), v_ref[pl.ds(c0, tq), h * w:(h + 1) * w])
            m_ref[h] = m_new

    def body(j, carry):
        tile(j, None)
        return carry

    lax.fori_loop(0, i, body, 0)

    row = lax.broadcasted_iota(jnp.int32, (2 * tq, tq), 0)
    col = lax.broadcasted_iota(jnp.int32, (2 * tq, tq), 1)
    row = jnp.where(row >= tq, row - tq, row)
    tile(i, (col // CHUNK) <= (row // CHUNK))

    lam = (jnp.exp(jnp.sum(lq1_ref[...] * lk1_ref[...], axis=1, keepdims=True))
           - jnp.exp(jnp.sum(lq2_ref[...] * lk2_ref[...], axis=1, keepdims=True)) + lambda_init)
    for h in range(hp):
        acc = acc_ref[h]
        l = jnp.sum(l_ref[h], axis=1, keepdims=True)
        o = acc[0:tq] / l[0:tq] - lam * (acc[tq:2 * tq] / l[tq:2 * tq])
        ms = jnp.mean(o * o, axis=-1, keepdims=True)
        o_ref[:, h * w:(h + 1) * w] = (o * lax.rsqrt(ms + EPS) * sw_ref[...] * (1.0 - lambda_init)).astype(o_ref.dtype)


def _attention(q_prep, k_t, proj, lq1, lk1, lq2, lk2, subln_w, lambda_init, bsz, s_len, tq, hp):
    t = q_prep.shape[0]
    nq = s_len // tq
    wblk = hp * ATT_V_DIM
    v_blk0 = (3 * D_MODEL) // wblk
    vec = lambda n: pl.BlockSpec((1, n), lambda b, h, i: (0, 0))
    return pl.pallas_call(
        functools.partial(_attn_kernel, tq=tq, hp=hp, lambda_init=lambda_init),
        grid=(bsz, ATT_HEADS // hp, nq),
        in_specs=[
            pl.BlockSpec((tq, wblk), lambda b, h, i: (b * nq + i, h)),
            pl.BlockSpec((1, wblk, s_len), lambda b, h, i: (b, h, 0)),
            pl.BlockSpec((s_len, wblk), lambda b, h, i: (b, v_blk0 + h)),
            vec(ATT_QK_DIM), vec(ATT_QK_DIM), vec(ATT_QK_DIM), vec(ATT_QK_DIM),
            vec(ATT_V_DIM),
        ],
        out_specs=pl.BlockSpec((tq, wblk), lambda b, h, i: (b * nq + i, h)),
        out_shape=jax.ShapeDtypeStruct((t, ATT_WIDTH), BF16),
        scratch_shapes=[
            pltpu.VMEM((hp, 2 * tq, LANES), F32),
            pltpu.VMEM((hp, 2 * tq, LANES), F32),
            pltpu.VMEM((hp, 2 * tq, ATT_V_DIM), F32),
        ],
        compiler_params=_cparams(("parallel", "parallel", "arbitrary")),
        name="attn",
    )(q_prep, k_t, proj, lq1, lk1, lq2, lk2, subln_w)


def _merge_kernel(ys_ref, ya_ref, gs_ref, ga_ref, x_ref, gb_ref, ws_ref, wa_ref, wo_ref, nw_ref, *rest,
                  with_router):
    if with_router:
        rw_ref, tril_ref, xo_ref, hn3_ref, w12_ref, pos_ref, cnt_ref, run_ref = rest
    else:
        xo_ref, hn_ref = rest
    gs = _sigmoid(gs_ref[...].astype(F32) + gb_ref[0:1, :])
    ga = _sigmoid(ga_ref[...].astype(F32) + gb_ref[1:2, :])
    merged = gs * _dot(ys_ref[...], ws_ref[...]) + ga * _dot(ya_ref[...], wa_ref[...])
    xn = x_ref[...] + _dot(merged.astype(BF16), wo_ref[...])
    xo_ref[...] = xn
    ms = jnp.mean(xn * xn, axis=-1, keepdims=True)
    hn = xn * lax.rsqrt(ms + EPS) * nw_ref[...]
    if not with_router:
        hn_ref[...] = hn.astype(hn_ref.dtype)
        return

    @pl.when(pl.program_id(0) == 0)
    def _():
        run_ref[...] = jnp.zeros_like(run_ref)

    for s in range(D_MODEL // LANES):
        hn3_ref[:, s, :] = hn[:, s * LANES:(s + 1) * LANES]

    h1, h2, _ = _split3(hn)
    logits = _dot(h1, rw_ref[0]) + _dot(h1, rw_ref[1]) + _dot(h2, rw_ref[0])
    lane = lax.broadcasted_iota(jnp.int32, logits.shape, 1)
    neg = jnp.float32(-jnp.inf)
    lg = jnp.where(lane < N_EXPERTS, logits, neg)
    m1 = jnp.max(lg, axis=1, keepdims=True)
    i1 = jnp.min(jnp.where(lg == m1, lane, LANES), axis=1, keepdims=True)
    lg2 = jnp.where(lane == i1, neg, lg)
    m2 = jnp.max(lg2, axis=1, keepdims=True)
    i2 = jnp.min(jnp.where(lg2 == m2, lane, LANES), axis=1, keepdims=True)
    e2 = jnp.exp(m2 - m1)
    w1 = 1.0 / (1.0 + e2)
    w2 = e2 / (1.0 + e2)
    w12_ref[...] = jnp.where(lane == 0, w1, 0.0) + jnp.where(lane == 1, w2, 0.0)

    sel = jnp.where(lane == i1, 1.0, 0.0) + jnp.where(lane == i2, 1.0, 0.0)
    rank = run_ref[...] + _dot(tril_ref[...], sel.astype(BF16))
    r1 = jnp.sum(jnp.where(lane == i1, rank, 0.0), axis=1, keepdims=True)
    r2 = jnp.sum(jnp.where(lane == i2, rank, 0.0), axis=1, keepdims=True)
    route = (jnp.where(lane == 0, i1.astype(F32), 0.0) + jnp.where(lane == 1, i2.astype(F32), 0.0)
             + jnp.where(lane == 2, r1, 0.0) + jnp.where(lane == 3, r2, 0.0))
    pos_ref[...] = jnp.transpose(route)[0:8, :].astype(jnp.int32)
    run_ref[...] = run_ref[...] + jnp.sum(sel, axis=0, keepdims=True)
    cnt_ref[...] = run_ref[...]


def _merge(y_ssd, y_att, proj, x2d, gate_b, ws, wa, wo, nw, router_w, tril, tm):
    t = x2d.shape[0]
    with_router = router_w is not None
    row = lambda w, blk: pl.BlockSpec((tm, w), lambda i: (i, blk))
    full = lambda shape: pl.BlockSpec(shape, lambda i: (0,) * len(shape))
    in_specs = [
        row(D_MODEL, 0), row(D_MODEL, 0), row(D_MODEL, 4), row(D_MODEL, 5), row(D_MODEL, 0),
        full((2, D_MODEL)), full((D_MODEL, D_MODEL)), full((D_MODEL, D_MODEL)), full((D_MODEL, D_MODEL)),
        full((1, D_MODEL)),
    ]
    args = [y_ssd, y_att, proj, proj, x2d, gate_b, ws, wa, wo, nw]
    scratch = []
    if with_router:
        in_specs += [full((2, D_MODEL, LANES)), full((tm, tm))]
        args += [router_w, tril]
        out_specs = [
            row(D_MODEL, 0),
            pl.BlockSpec((tm, D_MODEL // LANES, LANES), lambda i: (i, 0, 0)),
            row(LANES, 0),
            pl.BlockSpec((8, tm), lambda i: (0, i)),
            full((1, LANES)),
        ]
        out_shape = [
            jax.ShapeDtypeStruct((t, D_MODEL), F32),
            jax.ShapeDtypeStruct((t, D_MODEL // LANES, LANES), F32),
            jax.ShapeDtypeStruct((t, LANES), F32),
            jax.ShapeDtypeStruct((8, t), jnp.int32),
            jax.ShapeDtypeStruct((1, LANES), F32),
        ]
        scratch = [pltpu.VMEM((1, LANES), F32)]
    else:
        out_specs = [row(D_MODEL, 0), row(D_MODEL, 0)]
        out_shape = [jax.ShapeDtypeStruct((t, D_MODEL), F32), jax.ShapeDtypeStruct((t, D_MODEL), BF16)]
    return pl.pallas_call(
        functools.partial(_merge_kernel, with_router=with_router),
        grid=(t // tm,),
        in_specs=in_specs,
        out_specs=out_specs,
        out_shape=out_shape,
        scratch_shapes=scratch,
        compiler_params=_cparams(("arbitrary",) if with_router else ("parallel",)),
        name="merge_router" if with_router else "merge",
    )(*args)


def _ffn_kernel(hn_ref, x_ref, wg_ref, wu_ref, wd_ref, o_ref):
    f = pl.program_id(1)
    hn = hn_ref[...]
    h = (_silu(_dot(hn, wg_ref[...])) * _dot(hn, wu_ref[...])).astype(BF16)
    y = _dot(h, wd_ref[...])

    @pl.when(f == 0)
    def _():
        o_ref[...] = x_ref[...] + y

    @pl.when(f > 0)
    def _():
        o_ref[...] += y


def _ffn(hn, x2d, wg, wu, wd, tm, tf):
    t = x2d.shape[0]
    return pl.pallas_call(
        _ffn_kernel,
        grid=(t // tm, D_FF // tf),
        in_specs=[
            pl.BlockSpec((tm, D_MODEL), lambda i, f: (i, 0)),
            pl.BlockSpec((tm, D_MODEL), lambda i, f: (i, 0)),
            pl.BlockSpec((D_MODEL, tf), lambda i, f: (0, f)),
            pl.BlockSpec((D_MODEL, tf), lambda i, f: (0, f)),
            pl.BlockSpec((tf, D_MODEL), lambda i, f: (f, 0)),
        ],
        out_specs=pl.BlockSpec((tm, D_MODEL), lambda i, f: (i, 0)),
        out_shape=jax.ShapeDtypeStruct((t, D_MODEL), F32),
        compiler_params=_cparams(("parallel", "arbitrary")),
        name="ffn",
    )(hn, x2d, wg, wu, wd)


META_OFF, META_LAST, META_NVALID = 0, N_EXPERTS, 2 * N_EXPERTS


def _dispatch_kernel(route_ref, meta_ref, hn3_ref, zeros_ref, xs3_ref, sem, zsem, *, tm, n_tiles):
    i = pl.program_id(0)

    @pl.when(i == 0)
    def _():
        def ztile(row0):
            return pltpu.make_async_copy(zeros_ref, xs3_ref.at[pl.ds(pl.multiple_of(row0, MOE_TILE), MOE_TILE)], zsem)

        def each_zero_tile(fn):
            for e in range(N_EXPERTS):
                last = meta_ref[META_LAST + e]

                @pl.when(last >= 0)
                def _():
                    fn(ztile(last))
            for k in range(N_EXPERTS):
                g = meta_ref[META_NVALID] + k

                @pl.when(g < n_tiles)
                def _():
                    fn(ztile(g * MOE_TILE))

        each_zero_tile(lambda c: c.start())
        each_zero_tile(lambda c: c.wait())

    def body(t, carry):
        src = hn3_ref.at[t]
        d1 = meta_ref[META_OFF + route_ref[0, t]] + route_ref[2, t]
        d2 = meta_ref[META_OFF + route_ref[1, t]] + route_ref[3, t]
        pltpu.make_async_copy(src, xs3_ref.at[d1], sem).start()
        pltpu.make_async_copy(src, xs3_ref.at[d2], sem).start()
        return carry

    lax.fori_loop(0, tm, body, 0)
    for _ in range(2):
        pltpu.make_async_copy(hn3_ref, xs3_ref.at[pl.ds(0, tm)], sem).wait()


def _dispatch(route, meta, hn3, zeros_tile, tm, n_tiles):
    t = hn3.shape[0]
    return pl.pallas_call(
        functools.partial(_dispatch_kernel, tm=tm, n_tiles=n_tiles),
        grid=(t // tm,),
        in_specs=[
            pl.BlockSpec((8, tm), lambda i: (0, i), memory_space=pltpu.SMEM),
            pl.BlockSpec(memory_space=pltpu.SMEM),
            pl.BlockSpec((tm,) + hn3.shape[1:], lambda i: (i, 0, 0)),
            pl.BlockSpec(memory_space=pl.ANY),
        ],
        out_specs=pl.BlockSpec(memory_space=pl.ANY),
        out_shape=jax.ShapeDtypeStruct((n_tiles * MOE_TILE,) + hn3.shape[1:], F32),
        scratch_shapes=[pltpu.SemaphoreType.DMA, pltpu.SemaphoreType.DMA],
        compiler_params=_cparams(("arbitrary",)),
        name="moe_dispatch",
    )(route, meta, hn3, zeros_tile)


def _gffn_kernel(te_ref, nv_ref, x3_ref, wg_ref, wu_ref, wd_ref, y3_ref, xb_ref, acc_ref):
    g = pl.program_id(0)
    f = pl.program_id(1)
    nsl = D_MODEL // LANES

    @pl.when((g >= nv_ref[0]) & (f == 0))
    def _():
        y3_ref[...] = jnp.zeros_like(y3_ref)

    @pl.when(g < nv_ref[0])
    def _():
        @pl.when(f == 0)
        def _():
            xb_ref[...] = jnp.concatenate([x3_ref[:, s, :] for s in range(nsl)], axis=1).astype(BF16)

        xb = xb_ref[...]
        h = (_silu(_dot(xb, wg_ref[0])) * _dot(xb, wu_ref[0])).astype(BF16)
        y = _dot(h, wd_ref[0])

        @pl.when(f == 0)
        def _():
            acc_ref[...] = y

        @pl.when(f == pl.num_programs(1) - 1)
        def _():
            tot = acc_ref[...] + y
            for s in range(nsl):
                y3_ref[:, s, :] = tot[:, s * LANES:(s + 1) * LANES]


def _grouped_ffn(tile_expert, n_valid, xs3, wg, wu, wd, tf):
    n_tiles = tile_expert.shape[0]
    nsl = D_MODEL // LANES
    assert D_FF // tf == 2
    grid_spec = pltpu.PrefetchScalarGridSpec(
        num_scalar_prefetch=2,
        grid=(n_tiles, D_FF // tf),
        in_specs=[
            pl.BlockSpec((MOE_TILE, nsl, LANES), lambda g, f, te, nv: (g, 0, 0)),
            pl.BlockSpec((1, D_MODEL, tf), lambda g, f, te, nv: (te[g], 0, f)),
            pl.BlockSpec((1, D_MODEL, tf), lambda g, f, te, nv: (te[g], 0, f)),
            pl.BlockSpec((1, tf, D_MODEL), lambda g, f, te, nv: (te[g], f, 0)),
        ],
        out_specs=pl.BlockSpec((MOE_TILE, nsl, LANES), lambda g, f, te, nv: (g, 0, 0)),
        scratch_shapes=[pltpu.VMEM((MOE_TILE, D_MODEL), BF16), pltpu.VMEM((MOE_TILE, D_MODEL), F32)],
    )
    return pl.pallas_call(
        _gffn_kernel,
        grid_spec=grid_spec,
        out_shape=jax.ShapeDtypeStruct(xs3.shape, F32),
        compiler_params=_cparams(("arbitrary", "arbitrary")),
        name="moe_ffn",
    )(tile_expert, n_valid, xs3, wg, wu, wd)


def _combine_kernel(route_ref, meta_ref, w12_ref, x_ref, y3_ref, o_ref, buf_ref, sem, *, tm):
    def body(t, carry):
        d1 = meta_ref[META_OFF + route_ref[0, t]] + route_ref[2, t]
        d2 = meta_ref[META_OFF + route_ref[1, t]] + route_ref[3, t]
        pltpu.make_async_copy(y3_ref.at[d1], buf_ref.at[0, t], sem).start()
        pltpu.make_async_copy(y3_ref.at[d2], buf_ref.at[1, t], sem).start()
        return carry

    lax.fori_loop(0, tm, body, 0)
    for k in range(2):
        pltpu.make_async_copy(y3_ref.at[pl.ds(0, tm)], buf_ref.at[k], sem).wait()
    w = w12_ref[...]
    nsl = D_MODEL // LANES
    y1 = jnp.concatenate([buf_ref[0, :, s, :] for s in range(nsl)], axis=1)
    y2 = jnp.concatenate([buf_ref[1, :, s, :] for s in range(nsl)], axis=1)
    o_ref[...] = x_ref[...] + w[:, 0:1] * y1 + w[:, 1:2] * y2


def _combine(route, meta, w12, x2d, y3, tm):
    t = x2d.shape[0]
    nsl = D_MODEL // LANES
    return pl.pallas_call(
        functools.partial(_combine_kernel, tm=tm),
        grid=(t // tm,),
        in_specs=[
            pl.BlockSpec((8, tm), lambda i: (0, i), memory_space=pltpu.SMEM),
            pl.BlockSpec(memory_space=pltpu.SMEM),
            pl.BlockSpec((tm, LANES), lambda i: (i, 0)),
            pl.BlockSpec((tm, D_MODEL), lambda i: (i, 0)),
            pl.BlockSpec(memory_space=pl.ANY),
        ],
        out_specs=pl.BlockSpec((tm, D_MODEL), lambda i: (i, 0)),
        out_shape=jax.ShapeDtypeStruct((t, D_MODEL), F32),
        scratch_shapes=[pltpu.VMEM((2, tm, nsl, LANES), F32), pltpu.SemaphoreType.DMA],
        compiler_params=_cparams(("arbitrary",)),
        name="moe_combine",
    )(route, meta, w12, x2d, y3)


def _moe_tiles(cnt_f, n_tiles):
    cnt = cnt_f[0, :N_EXPERTS].astype(jnp.int32)
    per = (cnt + (MOE_TILE - 1)) // MOE_TILE
    ends = jnp.cumsum(per)
    starts = ends - per
    n_valid = ends[-1:]
    g = jnp.minimum(jnp.arange(n_tiles, dtype=jnp.int32), n_valid - 1)
    tile_expert = jnp.sum((g[:, None] >= ends[None, :]).astype(jnp.int32), axis=1)
    last = jnp.where(per > 0, (ends - 1) * MOE_TILE, -1)
    meta = jnp.concatenate([starts * MOE_TILE, last, n_valid]).astype(jnp.int32)
    return tile_expert.astype(jnp.int32), n_valid.astype(jnp.int32), jnp.pad(meta, (0, LANES - meta.shape[0]))


def _ssd_consts():
    e_mat = np.zeros((LANES, SSD_INNER), np.float32)
    for h in range(SSD_HEADS):
        e_mat[h, h * SSD_HEAD_DIM:(h + 1) * SSD_HEAD_DIM] = 1.0
    tril = np.tril(np.ones((CHUNK, CHUNK), np.float32))
    dmask = np.tile(np.eye(CHUNK, dtype=np.float32), (1, SSD_HEADS))
    trilmask = np.tile(tril, (1, SSD_HEADS))
    blk = np.arange(256) // SSD_HEAD_DIM
    bdmask = (blk[:, None] == blk[None, :]).astype(np.float32)
    return (jnp.asarray(e_mat, BF16), jnp.asarray(tril, BF16), jnp.asarray(dmask, F32),
            jnp.asarray(trilmask, F32), jnp.asarray(bdmask, BF16))


def _rope_tables(s_len):
    half = ROPE_DIM // 2
    inv = ROPE_THETA ** (-jnp.arange(0, ROPE_DIM, 2, dtype=F32) / ROPE_DIM)
    ang = jnp.arange(s_len, dtype=F32)[:, None] * inv[None, :]
    cos, sin = jnp.cos(ang), jnp.sin(ang)
    pad = jnp.zeros((s_len, ATT_QK_DIM - ROPE_DIM), F32)
    cos_c = jnp.concatenate([cos, cos, pad + 1.0], axis=1)
    sa_c = jnp.concatenate([-sin, jnp.zeros_like(sin), pad], axis=1)
    sb_c = jnp.concatenate([jnp.zeros_like(sin), sin, pad], axis=1)
    two = lambda a: jnp.concatenate([a, a], axis=1)
    return two(cos_c), two(sa_c), two(sb_c)


def _split2_w(w):
    w = jnp.pad(w, ((0, 0), (0, LANES - w.shape[1])))
    hi = w.astype(BF16)
    lo = (w - hi.astype(F32)).astype(BF16)
    return jnp.stack([hi, lo])


def _pick(t, prefs):
    for p in prefs:
        if t % p == 0:
            return p
    return t


def kernel(x, norm_mix_w, w_in, conv_w, conv_b, dt_bias, a_log, d_skip, ssd_norm_w, q_norm_w, k_norm_w,
           lambda_q1, lambda_k1, lambda_q2, lambda_k2, subln_w, gate_b, w_br_ssd, w_br_att, w_out,
           norm_ffn_w, ffn_w_gate, ffn_w_up, ffn_w_down, router_w, moe_w_gate, moe_w_up, moe_w_down):
    bsz, s_len, d = x.shape
    depth = w_in.shape[0]
    t = bsz * s_len
    assert d == D_MODEL and s_len % 512 == 0

    tm_proj = _pick(t, (1024, 512))
    tm_row = _pick(t, (512,))
    lb = 256
    tm_qk = 512
    tq, heads_per_step = 512, 2

    consts = _ssd_consts()
    tril_tok = jnp.asarray(np.tril(np.ones((tm_row, tm_row), np.float32), -1), BF16)
    zeros_tile = jnp.zeros((MOE_TILE, D_MODEL // LANES, LANES), F32)
    cos_t, sa_t, sb_t = _rope_tables(s_len)
    blk64 = np.arange(256) // ATT_QK_DIM
    gmat = jnp.asarray((blk64[:, None] == blk64[None, :]).astype(np.float32), BF16)

    o1 = SSD_INNER
    o2 = o1 + SSD_CONV_DIM
    o3 = o2 + SSD_HEADS
    o4 = o3 + ATT_WIDTH
    o5 = o4 + ATT_WIDTH
    o6 = o5 + ATT_WIDTH
    o7 = o6 + D_MODEL

    x2d = x.reshape(t, d)
    for i in range(depth):
        lambda_init = 0.8 - 0.6 * math.exp(-0.3 * i)
        wi = w_in[i]
        w_main = jnp.concatenate([wi[:, :o1], wi[:, o3:], wi[:, o1:o2]], axis=1).astype(BF16)
        w_dt = _split2_w(wi[:, o2:o3])
        proj, dt_raw = _in_proj(x2d, norm_mix_w[i][None, :], w_main, w_dt, tm_proj, SSD_CONV_DIM)

        dtb = jnp.pad(dt_bias[i], (0, LANES - SSD_HEADS))[None, :]
        aexp = jnp.repeat(-jnp.exp(a_log[i].astype(F32)), SSD_HEAD_DIM)[None, :]
        dexp = jnp.repeat(d_skip[i], SSD_HEAD_DIM)[None, :]
        y_ssd = _ssd(proj, dt_raw, conv_w[i], conv_b[i][None, :], dtb, aexp, dexp, ssd_norm_w[i][None, :],
                     consts, bsz, s_len, lb)

        qw = jnp.tile(q_norm_w[i], ATT_WIDTH // ATT_QK_DIM)[None, :]
        kw = jnp.tile(k_norm_w[i], ATT_WIDTH // ATT_QK_DIM)[None, :]
        q_prep, k_t = _qk_prep(proj, qw, kw, cos_t, sa_t, sb_t, gmat, bsz, s_len, tm_qk)
        y_att = _attention(q_prep, k_t, proj, lambda_q1[i][None, :], lambda_k1[i][None, :],
                           lambda_q2[i][None, :], lambda_k2[i][None, :], subln_w[i][None, :],
                           lambda_init, bsz, s_len, tq, heads_per_step)

        j = i // 2
        is_moe = i % 2 == 1
        rw = _split2_w(router_w[j]) if is_moe else None
        outs = _merge(y_ssd, y_att, proj, x2d, gate_b[i], w_br_ssd[i].astype(BF16), w_br_att[i].astype(BF16),
                      w_out[i].astype(BF16), norm_ffn_w[i][None, :], rw, tril_tok, tm_row)
        if is_moe:
            x2d, hn3, w12, route, cnt_f = outs
            n_tiles = (2 * t) // MOE_TILE + N_EXPERTS
            tile_e, n_valid, meta = _moe_tiles(cnt_f, n_tiles)
            xs3 = _dispatch(route, meta, hn3, zeros_tile, 1024, n_tiles)
            y3 = _grouped_ffn(tile_e, n_valid, xs3, moe_w_gate[j].astype(BF16),
                              moe_w_up[j].astype(BF16), moe_w_down[j].astype(BF16), D_FF // 2)
            x2d = _combine(route, meta, w12, x2d, y3, tm_row)
        else:
            x2d, hn = outs
            x2d = _ffn(hn, x2d, ffn_w_gate[j].astype(BF16), ffn_w_up[j].astype(BF16),
                       ffn_w_down[j].astype(BF16), tm_row, D_FF // 2)
    return x2d.reshape(bsz, s_len, d)
```

```python
import functools
import math

import numpy as np
import jax
import jax.numpy as jnp
from jax import lax
from jax.experimental import pallas as pl
from jax.experimental.pallas import tpu as pltpu

F32 = jnp.float32
BF16 = jnp.bfloat16

D_MODEL = 1024
CHUNK = 64
EPS = 1e-6
SSD_HEADS = 16
SSD_HEAD_DIM = 64
SSD_INNER = SSD_HEADS * SSD_HEAD_DIM
SSD_GROUPS = 2
SSD_STATE = 128
SSD_CONV = 4
SSD_CONV_DIM = SSD_INNER + 2 * SSD_GROUPS * SSD_STATE
ATT_HEADS = 8
ATT_QK_DIM = 64
ATT_V_DIM = 128
ATT_WIDTH = ATT_HEADS * ATT_V_DIM
ROPE_THETA = 500000.0
ROPE_DIM = ATT_QK_DIM // 4
D_FF = 2816
N_EXPERTS = 8
MOE_TILE = 512
LANES = 128
GROUP_W = SSD_INNER // SSD_GROUPS
MAIN_COLS = 6 * D_MODEL + SSD_CONV_DIM
VMEM_LIMIT = 56 * 1024 * 1024


def _cparams(sem):
    return pltpu.CompilerParams(dimension_semantics=sem, vmem_limit_bytes=VMEM_LIMIT)


def _split3(a):
    h1 = a.astype(BF16)
    r1 = a - h1.astype(F32)
    h2 = r1.astype(BF16)
    r2 = r1 - h2.astype(F32)
    return h1, h2, r2.astype(BF16)


def _dot(a, b):
    return jnp.dot(a, b, preferred_element_type=F32)


def _sigmoid(x):
    return 1.0 / (1.0 + jnp.exp(-x))


def _silu(x):
    return x * _sigmoid(x)


def _inproj_kernel(x_ref, nw_ref, w_ref, wdt_ref, o_ref, dt_ref, xn_ref):
    @pl.when(pl.program_id(1) == 0)
    def _():
        x = x_ref[...]
        ms = jnp.mean(x * x, axis=-1, keepdims=True)
        xn = x * lax.rsqrt(ms + EPS) * nw_ref[...]
        x1, x2, _ = _split3(xn)
        xn_ref[...] = x1
        dt_ref[...] = _dot(x1, wdt_ref[0]) + _dot(x1, wdt_ref[1]) + _dot(x2, wdt_ref[0])

    o_ref[...] = _dot(xn_ref[...], w_ref[...]).astype(o_ref.dtype)


def _in_proj(x2d, norm_w, w_main, w_dt, tm, tn):
    t, d = x2d.shape
    n = w_main.shape[1]
    return pl.pallas_call(
        _inproj_kernel,
        grid=(t // tm, n // tn),
        in_specs=[
            pl.BlockSpec((tm, d), lambda i, j: (i, 0)),
            pl.BlockSpec((1, d), lambda i, j: (0, 0)),
            pl.BlockSpec((d, tn), lambda i, j: (0, j)),
            pl.BlockSpec((2, d, LANES), lambda i, j: (0, 0, 0)),
        ],
        out_specs=[
            pl.BlockSpec((tm, tn), lambda i, j: (i, j)),
            pl.BlockSpec((tm, LANES), lambda i, j: (i, 0)),
        ],
        out_shape=[jax.ShapeDtypeStruct((t, n), BF16), jax.ShapeDtypeStruct((t, LANES), F32)],
        scratch_shapes=[pltpu.VMEM((tm, d), BF16)],
        compiler_params=_cparams(("parallel", "arbitrary")),
        name="in_proj",
    )(x2d, norm_w, w_main, w_dt)


def _ssd_kernel(xbc_ref, z_ref, dt_ref, convw_ref, convb_ref, dtb_ref, aexp_ref, dexp_ref, nw_ref,
                e_ref, tril_ref, dmask_ref, trilmask_ref, bdmask_ref,
                o_ref, xpad_ref, u_ref, dte_ref, state_ref, *, lb):
    @pl.when(pl.program_id(1) == 0)
    def _():
        xpad_ref[0:8, :] = jnp.zeros((8, SSD_CONV_DIM), F32)
        state_ref[...] = jnp.zeros_like(state_ref)

    xpad_ref[8:8 + lb, :] = xbc_ref[...].astype(F32)
    rp = 128
    for c in range(SSD_CONV_DIM // LANES):
        sl = slice(c * LANES, (c + 1) * LANES)
        for r in range(lb // rp):
            base = 8 - (SSD_CONV - 1) + r * rp
            acc = convb_ref[:, sl] + convw_ref[0:1, sl] * xpad_ref[pl.ds(base, rp), sl]
            for k in range(1, SSD_CONV):
                acc = acc + convw_ref[k:k + 1, sl] * xpad_ref[pl.ds(base + k, rp), sl]
            u_ref[r * rp:(r + 1) * rp, sl] = _silu(acc)
    xpad_ref[0:8, :] = xpad_ref[lb:lb + 8, :]

    t = dt_ref[...] + dtb_ref[...]
    dt = jnp.maximum(t, 0.0) + jnp.log(1.0 + jnp.exp(-jnp.abs(t)))
    d1, d2, d3 = _split3(dt)
    e = e_ref[...]
    dte_ref[...] = _dot(d1, e) + _dot(d2, e) + _dot(d3, e)

    tril = tril_ref[...]
    dmask = dmask_ref[...]
    trilmask = trilmask_ref[...] > 0.5
    bdmask = bdmask_ref[...]
    aexp = aexp_ref[...]
    dexp = dexp_ref[...]
    nw = nw_ref[...]

    def chunk(j, carry):
        r0 = pl.multiple_of(j * CHUNK, CHUNK)
        rows = pl.ds(r0, CHUNK)
        xs = u_ref[rows, 0:SSD_INNER]
        bm = u_ref[rows, SSD_INNER:SSD_INNER + SSD_GROUPS * SSD_STATE]
        cm = u_ref[rows, SSD_INNER + SSD_GROUPS * SSD_STATE:SSD_CONV_DIM]
        dte = dte_ref[rows, :]
        a1, a2, a3 = _split3(dte * aexp)
        cs = _dot(tril, a1) + _dot(tril, a2) + _dot(tril, a3)
        rowv = jnp.sum(cs * dmask, axis=0, keepdims=True)
        decay = jnp.exp(jnp.where(trilmask, cs - rowv, -1e30))
        a_end = cs[CHUNK - 1:CHUNK, :]
        to_end = jnp.exp(a_end - cs)
        xdt = xs * dte
        xdt_b = xdt.astype(BF16)
        xw_b = (xdt * to_end).astype(BF16)
        bm_b = bm.astype(BF16)
        cm_b = cm.astype(BF16)

        cb_parts = []
        yoff_parts = []
        for g in range(SSD_GROUPS):
            bg = bm_b[:, g * SSD_STATE:(g + 1) * SSD_STATE]
            cg = cm_b[:, g * SSD_STATE:(g + 1) * SSD_STATE]
            cb = lax.dot_general(cg, bg, (((1,), (1,)), ((), ())), preferred_element_type=F32)
            cb2 = jnp.concatenate([cb, cb], axis=1)
            cb_parts += [cb2] * (GROUP_W // LANES)
            st = state_ref[g]
            yoff_parts.append(_dot(cg, st.astype(BF16)))
            bg_t = jnp.transpose(bm[:, g * SSD_STATE:(g + 1) * SSD_STATE]).astype(BF16)
            upd = _dot(bg_t, xw_b[:, g * GROUP_W:(g + 1) * GROUP_W])
            state_ref[g] = jnp.exp(a_end[:, g * GROUP_W:(g + 1) * GROUP_W]) * st + upd
        m_b = (jnp.concatenate(cb_parts, axis=1) * decay).astype(BF16)
        y_off = jnp.concatenate(yoff_parts, axis=1) * jnp.exp(cs)

        yd_parts = []
        for q in range(SSD_INNER // 256):
            sl = slice(q * 256, (q + 1) * 256)
            rhs = jnp.concatenate([xdt_b[:, sl]] * 4, axis=0) * bdmask
            yd_parts.append(_dot(m_b[:, sl], rhs))
        y = jnp.concatenate(yd_parts, axis=1) + y_off + xs * dexp

        yz = y * _silu(z_ref[rows, :].astype(F32))
        outs = []
        for g in range(SSD_GROUPS):
            yg = yz[:, g * GROUP_W:(g + 1) * GROUP_W]
            ms = jnp.mean(yg * yg, axis=-1, keepdims=True)
            outs.append(yg * lax.rsqrt(ms + EPS))
        o_ref[rows, :] = (jnp.concatenate(outs, axis=1) * nw).astype(o_ref.dtype)
        return carry

    lax.fori_loop(0, lb // CHUNK, chunk, 0)


def _ssd(proj, dt_raw, conv_w, conv_b, dtb, aexp, dexp, norm_w, consts, bsz, s_len, lb):
    t = proj.shape[0]
    nblk = s_len // lb
    xbc_blk0 = (6 * D_MODEL) // SSD_CONV_DIM
    full = lambda shape: pl.BlockSpec(shape, lambda b, c: (0,) * len(shape))
    e_mat, tril, dmask, trilmask, bdmask = consts
    return pl.pallas_call(
        functools.partial(_ssd_kernel, lb=lb),
        grid=(bsz, nblk),
        in_specs=[
            pl.BlockSpec((lb, SSD_CONV_DIM), lambda b, c: (b * nblk + c, xbc_blk0)),
            pl.BlockSpec((lb, SSD_INNER), lambda b, c: (b * nblk + c, 0)),
            pl.BlockSpec((lb, LANES), lambda b, c: (b * nblk + c, 0)),
            full((SSD_CONV, SSD_CONV_DIM)),
            full((1, SSD_CONV_DIM)),
            full((1, LANES)),
            full((1, SSD_INNER)),
            full((1, SSD_INNER)),
            full((1, SSD_INNER)),
            full(e_mat.shape), full(tril.shape), full(dmask.shape), full(trilmask.shape), full(bdmask.shape),
        ],
        out_specs=pl.BlockSpec((lb, SSD_INNER), lambda b, c: (b * nblk + c, 0)),
        out_shape=jax.ShapeDtypeStruct((t, SSD_INNER), BF16),
        scratch_shapes=[
            pltpu.VMEM((lb + 8, SSD_CONV_DIM), F32),
            pltpu.VMEM((lb, SSD_CONV_DIM), F32),
            pltpu.VMEM((lb, SSD_INNER), F32),
            pltpu.VMEM((SSD_GROUPS, SSD_STATE, GROUP_W), F32),
        ],
        compiler_params=_cparams(("parallel", "arbitrary")),
        name="ssd",
    )(proj, proj, dt_raw, conv_w, conv_b, dtb, aexp, dexp, norm_w, e_mat, tril, dmask, trilmask, bdmask)


def _qkprep_kernel(q_ref, k_ref, qw_ref, kw_ref, cos_ref, sa_ref, sb_ref, g_ref, pup_ref, pdn_ref, qo_ref, kto_ref):
    sw = 256
    cos = jnp.concatenate([cos_ref[...]] * (sw // LANES), axis=1)
    sa = jnp.concatenate([sa_ref[...]] * (sw // LANES), axis=1)
    sb = jnp.concatenate([sb_ref[...]] * (sw // LANES), axis=1)
    gmat = g_ref[...]
    pup = pup_ref[...]
    pdn = pdn_ref[...]

    def prep(x, w):
        ss = _dot((x * x).astype(BF16), gmat)
        xn = x * lax.rsqrt(ss * (1.0 / ATT_QK_DIM) + EPS) * w
        xb = xn.astype(BF16)
        return xn * cos + _dot(xb, pup) * sa + _dot(xb, pdn) * sb

    for c in range(ATT_WIDTH // sw):
        sl = slice(c * sw, (c + 1) * sw)
        q = prep(q_ref[:, sl].astype(F32), qw_ref[:, sl]) * (ATT_QK_DIM ** -0.5 * math.log2(math.e))
        qo_ref[:, sl] = q.astype(qo_ref.dtype)
        k = prep(k_ref[:, sl].astype(F32), kw_ref[:, sl])
        kto_ref[0, sl, :] = jnp.transpose(k).astype(kto_ref.dtype)


def _qk_prep(proj, qw, kw, cos_t, sa_t, sb_t, gmat, pup, pdn, bsz, s_len, tm):
    t = proj.shape[0]
    nblk = s_len // tm
    full = lambda shape: pl.BlockSpec(shape, lambda b, c: (0,) * len(shape))
    return pl.pallas_call(
        _qkprep_kernel,
        grid=(bsz, nblk),
        in_specs=[
            pl.BlockSpec((tm, ATT_WIDTH), lambda b, c: (b * nblk + c, 1)),
            pl.BlockSpec((tm, ATT_WIDTH), lambda b, c: (b * nblk + c, 2)),
            full((1, ATT_WIDTH)), full((1, ATT_WIDTH)),
            pl.BlockSpec((tm, LANES), lambda b, c: (c, 0)),
            pl.BlockSpec((tm, LANES), lambda b, c: (c, 0)),
            pl.BlockSpec((tm, LANES), lambda b, c: (c, 0)),
            full((256, 256)), full((256, 256)), full((256, 256)),
        ],
        out_specs=[
            pl.BlockSpec((tm, ATT_WIDTH), lambda b, c: (b * nblk + c, 0)),
            pl.BlockSpec((1, ATT_WIDTH, tm), lambda b, c: (b, 0, c)),
        ],
        out_shape=[jax.ShapeDtypeStruct((t, ATT_WIDTH), BF16),
                   jax.ShapeDtypeStruct((bsz, ATT_WIDTH, s_len), BF16)],
        compiler_params=_cparams(("parallel", "parallel")),
        name="qk_prep",
    )(proj, proj, qw, kw, cos_t, sa_t, sb_t, gmat, pup, pdn)


def _attn_kernel(q_ref, kt_ref, v_ref, lq1_ref, lk1_ref, lq2_ref, lk2_ref, sw_ref, o_ref,
                 m_ref, l_ref, acc_ref, *, tq, hp, lambda_init):
    i = pl.program_id(2)
    w = ATT_V_DIM
    lane = lax.broadcasted_iota(jnp.int32, (tq, w), 1)
    qqs = []
    for h in range(hp):
        q = q_ref[:, h * w:(h + 1) * w]
        zero = jnp.zeros_like(q)
        qqs.append(jnp.concatenate([jnp.where(lane < ATT_QK_DIM, q, zero),
                                    jnp.where(lane >= ATT_QK_DIM, q, zero)], axis=0))

    m_ref[...] = jnp.full(m_ref.shape, -jnp.inf, F32)
    l_ref[...] = jnp.zeros(l_ref.shape, F32)
    acc_ref[...] = jnp.zeros(acc_ref.shape, F32)

    def tile(j, mask):
        c0 = pl.multiple_of(j * tq, tq)
        for h in range(hp):
            s = _dot(qqs[h], kt_ref[0, h * w:(h + 1) * w, pl.ds(c0, tq)])
            if mask is not None:
                s = jnp.where(mask, s, -jnp.inf)
            m_prev = m_ref[h]
            m_new = jnp.maximum(m_prev, jnp.max(s, axis=1, keepdims=True))
            alpha = jnp.exp2(m_prev - m_new)
            p = jnp.exp2(s - jnp.concatenate([m_new] * (tq // LANES), axis=1))
            psum = p[:, 0:LANES]
            for c in range(1, tq // LANES):
                psum = psum + p[:, c * LANES:(c + 1) * LANES]
            l_ref[h] = alpha * l_ref[h] + psum
            acc_ref[h] = alpha * acc_ref[h] + _dot(p.astype(BF16), v_ref[pl.ds(c0, tq), h * w:(h + 1) * w])
            m_ref[h] = m_new

    def body(j, carry):
        tile(j, None)
        return carry

    lax.fori_loop(0, i, body, 0)

    row = lax.broadcasted_iota(jnp.int32, (2 * tq, tq), 0)
    col = lax.broadcasted_iota(jnp.int32, (2 * tq, tq), 1)
    row = jnp.where(row >= tq, row - tq, row)
    tile(i, (col // CHUNK) <= (row // CHUNK))

    lam = (jnp.exp(jnp.sum(lq1_ref[...] * lk1_ref[...], axis=1, keepdims=True))
           - jnp.exp(jnp.sum(lq2_ref[...] * lk2_ref[...], axis=1, keepdims=True)) + lambda_init)
    for h in range(hp):
        acc = acc_ref[h]
        l = jnp.sum(l_ref[h], axis=1, keepdims=True)
        o = acc[0:tq] / l[0:tq] - lam * (acc[tq:2 * tq] / l[tq:2 * tq])
        ms = jnp.mean(o * o, axis=-1, keepdims=True)
        o_ref[:, h * w:(h + 1) * w] = (o * lax.rsqrt(ms + EPS) * sw_ref[...] * (1.0 - lambda_init)).astype(o_ref.dtype)


def _attention(q_prep, k_t, proj, lq1, lk1, lq2, lk2, subln_w, lambda_init, bsz, s_len, tq, hp):
    t = q_prep.shape[0]
    nq = s_len // tq
    wblk = hp * ATT_V_DIM
    v_blk0 = (3 * D_MODEL) // wblk
    vec = lambda n: pl.BlockSpec((1, n), lambda b, h, i: (0, 0))
    return pl.pallas_call(
        functools.partial(_attn_kernel, tq=tq, hp=hp, lambda_init=lambda_init),
        grid=(bsz, ATT_HEADS // hp, nq),
        in_specs=[
            pl.BlockSpec((tq, wblk), lambda b, h, i: (b * nq + i, h)),
            pl.BlockSpec((1, wblk, s_len), lambda b, h, i: (b, h, 0)),
            pl.BlockSpec((s_len, wblk), lambda b, h, i: (b, v_blk0 + h)),
            vec(ATT_QK_DIM), vec(ATT_QK_DIM), vec(ATT_QK_DIM), vec(ATT_QK_DIM),
            vec(ATT_V_DIM),
        ],
        out_specs=pl.BlockSpec((tq, wblk), lambda b, h, i: (b * nq + i, h)),
        out_shape=jax.ShapeDtypeStruct((t, ATT_WIDTH), BF16),
        scratch_shapes=[
            pltpu.VMEM((hp, 2 * tq, LANES), F32),
            pltpu.VMEM((hp, 2 * tq, LANES), F32),
            pltpu.VMEM((hp, 2 * tq, ATT_V_DIM), F32),
        ],
        compiler_params=_cparams(("parallel", "parallel", "arbitrary")),
        name="attn",
    )(q_prep, k_t, proj, lq1, lk1, lq2, lk2, subln_w)


def _merge_kernel(ys_ref, ya_ref, gs_ref, ga_ref, x_ref, gb_ref, ws_ref, wa_ref, wo_ref, nw_ref, *rest,
                  with_router):
    if with_router:
        rw_ref, tril_ref, xo_ref, hn3_ref, w12_ref, pos_ref, cnt_ref, run_ref = rest
    else:
        xo_ref, hn_ref = rest
    gs = _sigmoid(gs_ref[...].astype(F32) + gb_ref[0:1, :])
    ga = _sigmoid(ga_ref[...].astype(F32) + gb_ref[1:2, :])
    merged = gs * _dot(ys_ref[...], ws_ref[...]) + ga * _dot(ya_ref[...], wa_ref[...])
    xn = x_ref[...] + _dot(merged.astype(BF16), wo_ref[...])
    xo_ref[...] = xn
    ms = jnp.mean(xn * xn, axis=-1, keepdims=True)
    hn = xn * lax.rsqrt(ms + EPS) * nw_ref[...]
    if not with_router:
        hn_ref[...] = hn.astype(hn_ref.dtype)
        return

    @pl.when(pl.program_id(0) == 0)
    def _():
        run_ref[...] = jnp.zeros_like(run_ref)

    hn3_ref[...] = hn

    h1, h2, _ = _split3(hn)
    logits = _dot(h1, rw_ref[0]) + _dot(h1, rw_ref[1]) + _dot(h2, rw_ref[0])
    lane = lax.broadcasted_iota(jnp.int32, logits.shape, 1)
    neg = jnp.float32(-jnp.inf)
    lg = jnp.where(lane < N_EXPERTS, logits, neg)
    m1 = jnp.max(lg, axis=1, keepdims=True)
    i1 = jnp.min(jnp.where(lg == m1, lane, LANES), axis=1, keepdims=True)
    lg2 = jnp.where(lane == i1, neg, lg)
    m2 = jnp.max(lg2, axis=1, keepdims=True)
    i2 = jnp.min(jnp.where(lg2 == m2, lane, LANES), axis=1, keepdims=True)
    e2 = jnp.exp(m2 - m1)
    w1 = 1.0 / (1.0 + e2)
    w2 = e2 / (1.0 + e2)
    w12_ref[...] = jnp.where(lane == 0, w1, 0.0) + jnp.where(lane == 1, w2, 0.0)

    sel = jnp.where(lane == i1, 1.0, 0.0) + jnp.where(lane == i2, 1.0, 0.0)
    rank = run_ref[...] + _dot(tril_ref[...], sel.astype(BF16))
    r1 = jnp.sum(jnp.where(lane == i1, rank, 0.0), axis=1, keepdims=True)
    r2 = jnp.sum(jnp.where(lane == i2, rank, 0.0), axis=1, keepdims=True)
    route = (jnp.where(lane == 0, i1.astype(F32), 0.0) + jnp.where(lane == 1, i2.astype(F32), 0.0)
             + jnp.where(lane == 2, r1, 0.0) + jnp.where(lane == 3, r2, 0.0))
    pos_ref[...] = jnp.transpose(route)[0:8, :].astype(jnp.int32)
    run_ref[...] = run_ref[...] + jnp.sum(sel, axis=0, keepdims=True)
    cnt_ref[...] = run_ref[...]


def _merge(y_ssd, y_att, proj, x2d, gate_b, ws, wa, wo, nw, router_w, tril, tm):
    t = x2d.shape[0]
    with_router = router_w is not None
    row = lambda w, blk: pl.BlockSpec((tm, w), lambda i: (i, blk))
    full = lambda shape: pl.BlockSpec(shape, lambda i: (0,) * len(shape))
    in_specs = [
        row(D_MODEL, 0), row(D_MODEL, 0), row(D_MODEL, 4), row(D_MODEL, 5), row(D_MODEL, 0),
        full((2, D_MODEL)), full((D_MODEL, D_MODEL)), full((D_MODEL, D_MODEL)), full((D_MODEL, D_MODEL)),
        full((1, D_MODEL)),
    ]
    args = [y_ssd, y_att, proj, proj, x2d, gate_b, ws, wa, wo, nw]
    scratch = []
    if with_router:
        in_specs += [full((2, D_MODEL, LANES)), full((tm, tm))]
        args += [router_w, tril]
        out_specs = [
            row(D_MODEL, 0),
            row(D_MODEL, 0),
            row(LANES, 0),
            pl.BlockSpec((8, tm), lambda i: (0, i)),
            full((1, LANES)),
        ]
        out_shape = [
            jax.ShapeDtypeStruct((t, D_MODEL), F32),
            jax.ShapeDtypeStruct((t, D_MODEL), F32),
            jax.ShapeDtypeStruct((t, LANES), F32),
            jax.ShapeDtypeStruct((8, t), jnp.int32),
            jax.ShapeDtypeStruct((1, LANES), F32),
        ]
        scratch = [pltpu.VMEM((1, LANES), F32)]
    else:
        out_specs = [row(D_MODEL, 0), row(D_MODEL, 0)]
        out_shape = [jax.ShapeDtypeStruct((t, D_MODEL), F32), jax.ShapeDtypeStruct((t, D_MODEL), BF16)]
    return pl.pallas_call(
        functools.partial(_merge_kernel, with_router=with_router),
        grid=(t // tm,),
        in_specs=in_specs,
        out_specs=out_specs,
        out_shape=out_shape,
        scratch_shapes=scratch,
        compiler_params=_cparams(("arbitrary",) if with_router else ("parallel",)),
        name="merge_router" if with_router else "merge",
    )(*args)


def _ffn_kernel(hn_ref, x_ref, wg_ref, wu_ref, wd_ref, o_ref):
    f = pl.program_id(1)
    hn = hn_ref[...]
    h = (_silu(_dot(hn, wg_ref[...])) * _dot(hn, wu_ref[...])).astype(BF16)
    y = _dot(h, wd_ref[...])

    @pl.when(f == 0)
    def _():
        o_ref[...] = x_ref[...] + y

    @pl.when(f > 0)
    def _():
        o_ref[...] += y


def _ffn(hn, x2d, wg, wu, wd, tm, tf):
    t = x2d.shape[0]
    return pl.pallas_call(
        _ffn_kernel,
        grid=(t // tm, D_FF // tf),
        in_specs=[
            pl.BlockSpec((tm, D_MODEL), lambda i, f: (i, 0)),
            pl.BlockSpec((tm, D_MODEL), lambda i, f: (i, 0)),
            pl.BlockSpec((D_MODEL, tf), lambda i, f: (0, f)),
            pl.BlockSpec((D_MODEL, tf), lambda i, f: (0, f)),
            pl.BlockSpec((tf, D_MODEL), lambda i, f: (f, 0)),
        ],
        out_specs=pl.BlockSpec((tm, D_MODEL), lambda i, f: (i, 0)),
        out_shape=jax.ShapeDtypeStruct((t, D_MODEL), F32),
        compiler_params=_cparams(("parallel", "arbitrary")),
        name="ffn",
    )(hn, x2d, wg, wu, wd)


META_OFF, META_LAST, META_NVALID = 0, N_EXPERTS, 2 * N_EXPERTS


def _dispatch_kernel(route_ref, meta_ref, hn3_ref, zeros_ref, xs3_ref, sem, zsem, *, tm, n_tiles):
    i = pl.program_id(0)

    @pl.when(i == 0)
    def _():
        def ztile(row0):
            return pltpu.make_async_copy(zeros_ref, xs3_ref.at[pl.ds(pl.multiple_of(row0, MOE_TILE), MOE_TILE)], zsem)

        def each_zero_tile(fn):
            for e in range(N_EXPERTS):
                last = meta_ref[META_LAST + e]

                @pl.when(last >= 0)
                def _():
                    fn(ztile(last))
            for k in range(N_EXPERTS):
                g = meta_ref[META_NVALID] + k

                @pl.when(g < n_tiles)
                def _():
                    fn(ztile(g * MOE_TILE))

        each_zero_tile(lambda c: c.start())
        each_zero_tile(lambda c: c.wait())

    def body(t, carry):
        src = hn3_ref.at[pl.ds(t, 1)]
        d1 = meta_ref[META_OFF + route_ref[0, t]] + route_ref[2, t]
        d2 = meta_ref[META_OFF + route_ref[1, t]] + route_ref[3, t]
        pltpu.make_async_copy(src, xs3_ref.at[pl.ds(d1, 1)], sem).start()
        pltpu.make_async_copy(src, xs3_ref.at[pl.ds(d2, 1)], sem).start(priority=1)
        return carry

    lax.fori_loop(0, tm, body, 0)
    for _ in range(2):
        pltpu.make_async_copy(hn3_ref, xs3_ref.at[pl.ds(0, tm)], sem).wait()


def _dispatch(route, meta, hn3, zeros_tile, tm, n_tiles):
    t = hn3.shape[0]
    return pl.pallas_call(
        functools.partial(_dispatch_kernel, tm=tm, n_tiles=n_tiles),
        grid=(t // tm,),
        in_specs=[
            pl.BlockSpec((8, tm), lambda i: (0, i), memory_space=pltpu.SMEM),
            pl.BlockSpec(memory_space=pltpu.SMEM),
            pl.BlockSpec((tm,) + hn3.shape[1:], lambda i: (i, 0)),
            pl.BlockSpec(memory_space=pl.ANY),
        ],
        out_specs=pl.BlockSpec(memory_space=pl.ANY),
        out_shape=jax.ShapeDtypeStruct((n_tiles * MOE_TILE,) + hn3.shape[1:], F32),
        scratch_shapes=[pltpu.SemaphoreType.DMA, pltpu.SemaphoreType.DMA],
        compiler_params=_cparams(("arbitrary",)),
        name="moe_dispatch",
    )(route, meta, hn3, zeros_tile)


def _gffn_kernel(te_ref, nv_ref, x3_ref, wg_ref, wu_ref, wd_ref, y3_ref, xb_ref, acc_ref):
    g = pl.program_id(0)
    f = pl.program_id(1)
    nsl = D_MODEL // LANES

    @pl.when((g >= nv_ref[0]) & (f == 0))
    def _():
        y3_ref[...] = jnp.zeros_like(y3_ref)

    @pl.when(g < nv_ref[0])
    def _():
        @pl.when(f == 0)
        def _():
            xb_ref[...] = x3_ref[...].astype(BF16)

        xb = xb_ref[...]
        h = (_silu(_dot(xb, wg_ref[0])) * _dot(xb, wu_ref[0])).astype(BF16)
        y = _dot(h, wd_ref[0])

        @pl.when(f == 0)
        def _():
            acc_ref[...] = y

        @pl.when(f == pl.num_programs(1) - 1)
        def _():
            y3_ref[...] = acc_ref[...] + y


def _grouped_ffn(tile_expert, n_valid, xs3, wg, wu, wd, tf):
    n_tiles = tile_expert.shape[0]
    nsl = D_MODEL // LANES
    assert D_FF // tf == 2
    grid_spec = pltpu.PrefetchScalarGridSpec(
        num_scalar_prefetch=2,
        grid=(n_tiles, D_FF // tf),
        in_specs=[
            pl.BlockSpec((MOE_TILE, D_MODEL), lambda g, f, te, nv: (g, 0)),
            pl.BlockSpec((1, D_MODEL, tf), lambda g, f, te, nv: (te[g], 0, f)),
            pl.BlockSpec((1, D_MODEL, tf), lambda g, f, te, nv: (te[g], 0, f)),
            pl.BlockSpec((1, tf, D_MODEL), lambda g, f, te, nv: (te[g], f, 0)),
        ],
        out_specs=pl.BlockSpec((MOE_TILE, D_MODEL), lambda g, f, te, nv: (g, 0)),
        scratch_shapes=[pltpu.VMEM((MOE_TILE, D_MODEL), BF16), pltpu.VMEM((MOE_TILE, D_MODEL), F32)],
    )
    return pl.pallas_call(
        _gffn_kernel,
        grid_spec=grid_spec,
        out_shape=jax.ShapeDtypeStruct(xs3.shape, F32),
        compiler_params=_cparams(("arbitrary", "arbitrary")),
        name="moe_ffn",
    )(tile_expert, n_valid, xs3, wg, wu, wd)


def _combine_kernel(route_ref, meta_ref, w12_ref, x_ref, y3_ref, o_ref, buf_ref, sem, *, tm):
    def body(t, carry):
        d1 = meta_ref[META_OFF + route_ref[0, t]] + route_ref[2, t]
        d2 = meta_ref[META_OFF + route_ref[1, t]] + route_ref[3, t]
        pltpu.make_async_copy(y3_ref.at[pl.ds(d1, 1)], buf_ref.at[0, pl.ds(t, 1)], sem).start()
        pltpu.make_async_copy(y3_ref.at[pl.ds(d2, 1)], buf_ref.at[1, pl.ds(t, 1)], sem).start(priority=1)
        return carry

    lax.fori_loop(0, tm, body, 0)
    for k in range(2):
        pltpu.make_async_copy(y3_ref.at[pl.ds(0, tm)], buf_ref.at[k], sem).wait()
    w = w12_ref[...]
    nsl = D_MODEL // LANES
    o_ref[...] = x_ref[...] + w[:, 0:1] * buf_ref[0] + w[:, 1:2] * buf_ref[1]


def _combine(route, meta, w12, x2d, y3, tm):
    t = x2d.shape[0]
    nsl = D_MODEL // LANES
    return pl.pallas_call(
        functools.partial(_combine_kernel, tm=tm),
        grid=(t // tm,),
        in_specs=[
            pl.BlockSpec((8, tm), lambda i: (0, i), memory_space=pltpu.SMEM),
            pl.BlockSpec(memory_space=pltpu.SMEM),
            pl.BlockSpec((tm, LANES), lambda i: (i, 0)),
            pl.BlockSpec((tm, D_MODEL), lambda i: (i, 0)),
            pl.BlockSpec(memory_space=pl.ANY),
        ],
        out_specs=pl.BlockSpec((tm, D_MODEL), lambda i: (i, 0)),
        out_shape=jax.ShapeDtypeStruct((t, D_MODEL), F32),
        scratch_shapes=[pltpu.VMEM((2, tm, D_MODEL), F32), pltpu.SemaphoreType.DMA],
        compiler_params=_cparams(("arbitrary",)),
        name="moe_combine",
    )(route, meta, w12, x2d, y3)


def _moe_tiles(cnt_f, n_tiles):
    cnt = cnt_f[0, :N_EXPERTS].astype(jnp.int32)
    per = (cnt + (MOE_TILE - 1)) // MOE_TILE
    ends = jnp.cumsum(per)
    starts = ends - per
    n_valid = ends[-1:]
    g = jnp.minimum(jnp.arange(n_tiles, dtype=jnp.int32), n_valid - 1)
    tile_expert = jnp.sum((g[:, None] >= ends[None, :]).astype(jnp.int32), axis=1)
    last = jnp.where(per > 0, (ends - 1) * MOE_TILE, -1)
    meta = jnp.concatenate([starts * MOE_TILE, last, n_valid]).astype(jnp.int32)
    return tile_expert.astype(jnp.int32), n_valid.astype(jnp.int32), jnp.pad(meta, (0, LANES - meta.shape[0]))


def _ssd_consts():
    e_mat = np.zeros((LANES, SSD_INNER), np.float32)
    for h in range(SSD_HEADS):
        e_mat[h, h * SSD_HEAD_DIM:(h + 1) * SSD_HEAD_DIM] = 1.0
    tril = np.tril(np.ones((CHUNK, CHUNK), np.float32))
    dmask = np.tile(np.eye(CHUNK, dtype=np.float32), (1, SSD_HEADS))
    trilmask = np.tile(tril, (1, SSD_HEADS))
    blk = np.arange(256) // SSD_HEAD_DIM
    bdmask = (blk[:, None] == blk[None, :]).astype(np.float32)
    return (jnp.asarray(e_mat, BF16), jnp.asarray(tril, BF16), jnp.asarray(dmask, F32),
            jnp.asarray(trilmask, F32), jnp.asarray(bdmask, BF16))


def _rope_tables(s_len):
    half = ROPE_DIM // 2
    inv = ROPE_THETA ** (-jnp.arange(0, ROPE_DIM, 2, dtype=F32) / ROPE_DIM)
    ang = jnp.arange(s_len, dtype=F32)[:, None] * inv[None, :]
    cos, sin = jnp.cos(ang), jnp.sin(ang)
    pad = jnp.zeros((s_len, ATT_QK_DIM - ROPE_DIM), F32)
    cos_c = jnp.concatenate([cos, cos, pad + 1.0], axis=1)
    sa_c = jnp.concatenate([-sin, jnp.zeros_like(sin), pad], axis=1)
    sb_c = jnp.concatenate([jnp.zeros_like(sin), sin, pad], axis=1)
    two = lambda a: jnp.concatenate([a, a], axis=1)
    return two(cos_c), two(sa_c), two(sb_c)


def _split2_w(w):
    w = jnp.pad(w, ((0, 0), (0, LANES - w.shape[1])))
    hi = w.astype(BF16)
    lo = (w - hi.astype(F32)).astype(BF16)
    return jnp.stack([hi, lo])


def _pick(t, prefs):
    for p in prefs:
        if t % p == 0:
            return p
    return t


def kernel(x, norm_mix_w, w_in, conv_w, conv_b, dt_bias, a_log, d_skip, ssd_norm_w, q_norm_w, k_norm_w,
           lambda_q1, lambda_k1, lambda_q2, lambda_k2, subln_w, gate_b, w_br_ssd, w_br_att, w_out,
           norm_ffn_w, ffn_w_gate, ffn_w_up, ffn_w_down, router_w, moe_w_gate, moe_w_up, moe_w_down):
    bsz, s_len, d = x.shape
    depth = w_in.shape[0]
    t = bsz * s_len
    assert d == D_MODEL and s_len % 512 == 0

    tm_proj = _pick(t, (1024, 512))
    tm_row = _pick(t, (512,))
    lb = 256
    tm_qk = 512
    tq, heads_per_step = 512, 4

    consts = _ssd_consts()
    tril_tok = jnp.asarray(np.tril(np.ones((tm_row, tm_row), np.float32), -1), BF16)
    zeros_tile = jnp.zeros((MOE_TILE, D_MODEL), F32)
    cos_t, sa_t, sb_t = _rope_tables(s_len)
    blk64 = np.arange(256) // ATT_QK_DIM
    gmat = jnp.asarray((blk64[:, None] == blk64[None, :]).astype(np.float32), BF16)
    pup = jnp.asarray(np.eye(256, k=-(ROPE_DIM // 2), dtype=np.float32), BF16)
    pdn = jnp.asarray(np.eye(256, k=ROPE_DIM // 2, dtype=np.float32), BF16)

    o1 = SSD_INNER
    o2 = o1 + SSD_CONV_DIM
    o3 = o2 + SSD_HEADS
    o4 = o3 + ATT_WIDTH
    o5 = o4 + ATT_WIDTH
    o6 = o5 + ATT_WIDTH
    o7 = o6 + D_MODEL

    x2d = x.reshape(t, d)
    for i in range(depth):
        lambda_init = 0.8 - 0.6 * math.exp(-0.3 * i)
        wi = w_in[i]
        w_main = jnp.concatenate([wi[:, :o1], wi[:, o3:], wi[:, o1:o2]], axis=1).astype(BF16)
        w_dt = _split2_w(wi[:, o2:o3])
        proj, dt_raw = _in_proj(x2d, norm_mix_w[i][None, :], w_main, w_dt, tm_proj, SSD_CONV_DIM)

        dtb = jnp.pad(dt_bias[i], (0, LANES - SSD_HEADS))[None, :]
        aexp = jnp.repeat(-jnp.exp(a_log[i].astype(F32)), SSD_HEAD_DIM)[None, :]
        dexp = jnp.repeat(d_skip[i], SSD_HEAD_DIM)[None, :]
        y_ssd = _ssd(proj, dt_raw, conv_w[i], conv_b[i][None, :], dtb, aexp, dexp, ssd_norm_w[i][None, :],
                     consts, bsz, s_len, lb)

        qw = jnp.tile(q_norm_w[i], ATT_WIDTH // ATT_QK_DIM)[None, :]
        kw = jnp.tile(k_norm_w[i], ATT_WIDTH // ATT_QK_DIM)[None, :]
        q_prep, k_t = _qk_prep(proj, qw, kw, cos_t, sa_t, sb_t, gmat, pup, pdn, bsz, s_len, tm_qk)
        y_att = _attention(q_prep, k_t, proj, lambda_q1[i][None, :], lambda_k1[i][None, :],
                           lambda_q2[i][None, :], lambda_k2[i][None, :], subln_w[i][None, :],
                           lambda_init, bsz, s_len, tq, heads_per_step)

        j = i // 2
        is_moe = i % 2 == 1
        rw = _split2_w(router_w[j]) if is_moe else None
        outs = _merge(y_ssd, y_att, proj, x2d, gate_b[i], w_br_ssd[i].astype(BF16), w_br_att[i].astype(BF16),
                      w_out[i].astype(BF16), norm_ffn_w[i][None, :], rw, tril_tok, tm_row)
        if is_moe:
            x2d, hn3, w12, route, cnt_f = outs
            n_tiles = (2 * t) // MOE_TILE + N_EXPERTS
            tile_e, n_valid, meta = _moe_tiles(cnt_f, n_tiles)
            xs3 = _dispatch(route, meta, hn3, zeros_tile, 1024, n_tiles)
            y3 = _grouped_ffn(tile_e, n_valid, xs3, moe_w_gate[j].astype(BF16),
                              moe_w_up[j].astype(BF16), moe_w_down[j].astype(BF16), D_FF // 2)
            x2d = _combine(route, meta, w12, x2d, y3, tm_row)
        else:
            x2d, hn = outs
            x2d = _ffn(hn, x2d, ffn_w_gate[j].astype(BF16), ffn_w_up[j].astype(BF16),
                       ffn_w_down[j].astype(BF16), tm_row, D_FF // 2)
    return x2d.reshape(bsz, s_len, d)
```

```python
import functools
import math

import numpy as np
import jax
import jax.numpy as jnp
from jax import lax
from jax.experimental import pallas as pl
from jax.experimental.pallas import tpu as pltpu

F32 = jnp.float32
BF16 = jnp.bfloat16

D_MODEL = 1024
CHUNK = 64
EPS = 1e-6
SSD_HEADS = 16
SSD_HEAD_DIM = 64
SSD_INNER = SSD_HEADS * SSD_HEAD_DIM
SSD_GROUPS = 2
SSD_STATE = 128
SSD_CONV = 4
SSD_CONV_DIM = SSD_INNER + 2 * SSD_GROUPS * SSD_STATE
ATT_HEADS = 8
ATT_QK_DIM = 64
ATT_V_DIM = 128
ATT_WIDTH = ATT_HEADS * ATT_V_DIM
ROPE_THETA = 500000.0
ROPE_DIM = ATT_QK_DIM // 4
D_FF = 2816
N_EXPERTS = 8
MOE_TILE = 512
LANES = 128
GROUP_W = SSD_INNER // SSD_GROUPS
MAIN_COLS = 6 * D_MODEL + SSD_CONV_DIM
VMEM_LIMIT = 56 * 1024 * 1024


def _cparams(sem):
    return pltpu.CompilerParams(dimension_semantics=sem, vmem_limit_bytes=VMEM_LIMIT)


def _split3(a):
    h1 = a.astype(BF16)
    r1 = a - h1.astype(F32)
    h2 = r1.astype(BF16)
    r2 = r1 - h2.astype(F32)
    return h1, h2, r2.astype(BF16)


def _dot(a, b):
    return jnp.dot(a, b, preferred_element_type=F32)


def _sigmoid(x):
    return 1.0 / (1.0 + jnp.exp(-x))


def _silu(x):
    return x * _sigmoid(x)


def _inproj_kernel(x_ref, nw_ref, w_ref, wdt_ref, o_ref, dt_ref, xn_ref):
    @pl.when(pl.program_id(1) == 0)
    def _():
        x = x_ref[...]
        ms = jnp.mean(x * x, axis=-1, keepdims=True)
        xn = x * lax.rsqrt(ms + EPS) * nw_ref[...]
        x1, x2, _ = _split3(xn)
        xn_ref[...] = x1
        dt_ref[...] = _dot(x1, wdt_ref[0]) + _dot(x1, wdt_ref[1]) + _dot(x2, wdt_ref[0])

    o_ref[...] = _dot(xn_ref[...], w_ref[...]).astype(o_ref.dtype)


def _in_proj(x2d, norm_w, w_main, w_dt, tm, tn):
    t, d = x2d.shape
    n = w_main.shape[1]
    return pl.pallas_call(
        _inproj_kernel,
        grid=(t // tm, n // tn),
        in_specs=[
            pl.BlockSpec((tm, d), lambda i, j: (i, 0)),
            pl.BlockSpec((1, d), lambda i, j: (0, 0)),
            pl.BlockSpec((d, tn), lambda i, j: (0, j)),
            pl.BlockSpec((2, d, LANES), lambda i, j: (0, 0, 0)),
        ],
        out_specs=[
            pl.BlockSpec((tm, tn), lambda i, j: (i, j)),
            pl.BlockSpec((tm, LANES), lambda i, j: (i, 0)),
        ],
        out_shape=[jax.ShapeDtypeStruct((t, n), BF16), jax.ShapeDtypeStruct((t, LANES), F32)],
        scratch_shapes=[pltpu.VMEM((tm, d), BF16)],
        compiler_params=_cparams(("parallel", "arbitrary")),
        name="in_proj",
    )(x2d, norm_w, w_main, w_dt)


def _ssd_kernel(xbc_ref, z_ref, dt_ref, convw_ref, convb_ref, dtb_ref, aexp_ref, dexp_ref, nw_ref,
                e_ref, tril_ref, dmask_ref, trilmask_ref, bdmask_ref,
                o_ref, xpad_ref, u_ref, dte_ref, state_ref, *, lb):
    @pl.when(pl.program_id(1) == 0)
    def _():
        xpad_ref[0:8, :] = jnp.zeros((8, SSD_CONV_DIM), F32)
        state_ref[...] = jnp.zeros_like(state_ref)

    xpad_ref[8:8 + lb, :] = xbc_ref[...].astype(F32)
    rp = 128
    for c in range(SSD_CONV_DIM // LANES):
        sl = slice(c * LANES, (c + 1) * LANES)
        for r in range(lb // rp):
            base = 8 - (SSD_CONV - 1) + r * rp
            acc = convb_ref[:, sl] + convw_ref[0:1, sl] * xpad_ref[pl.ds(base, rp), sl]
            for k in range(1, SSD_CONV):
                acc = acc + convw_ref[k:k + 1, sl] * xpad_ref[pl.ds(base + k, rp), sl]
            u_ref[r * rp:(r + 1) * rp, sl] = _silu(acc)
    xpad_ref[0:8, :] = xpad_ref[lb:lb + 8, :]

    t = dt_ref[...] + dtb_ref[...]
    dt = jnp.maximum(t, 0.0) + jnp.log(1.0 + jnp.exp(-jnp.abs(t)))
    d1, d2, d3 = _split3(dt)
    e = e_ref[...]
    dte_ref[...] = _dot(d1, e) + _dot(d2, e) + _dot(d3, e)

    tril = tril_ref[...]
    dmask = dmask_ref[...]
    trilmask = trilmask_ref[...] > 0.5
    bdmask = bdmask_ref[...]
    aexp = aexp_ref[...]
    dexp = dexp_ref[...]
    nw = nw_ref[...]

    def chunk(j, carry):
        r0 = pl.multiple_of(j * CHUNK, CHUNK)
        rows = pl.ds(r0, CHUNK)
        xs = u_ref[rows, 0:SSD_INNER]
        bm = u_ref[rows, SSD_INNER:SSD_INNER + SSD_GROUPS * SSD_STATE]
        cm = u_ref[rows, SSD_INNER + SSD_GROUPS * SSD_STATE:SSD_CONV_DIM]
        dte = dte_ref[rows, :]
        a1, a2, a3 = _split3(dte * aexp)
        cs = _dot(tril, a1) + _dot(tril, a2) + _dot(tril, a3)
        rowv = jnp.sum(cs * dmask, axis=0, keepdims=True)
        decay = jnp.exp(jnp.where(trilmask, cs - rowv, -1e30))
        a_end = cs[CHUNK - 1:CHUNK, :]
        to_end = jnp.exp(a_end - cs)
        xdt = xs * dte
        xdt_b = xdt.astype(BF16)
        xw_b = (xdt * to_end).astype(BF16)
        bm_b = bm.astype(BF16)
        cm_b = cm.astype(BF16)

        cb_parts = []
        yoff_parts = []
        for g in range(SSD_GROUPS):
            bg = bm_b[:, g * SSD_STATE:(g + 1) * SSD_STATE]
            cg = cm_b[:, g * SSD_STATE:(g + 1) * SSD_STATE]
            cb = lax.dot_general(cg, bg, (((1,), (1,)), ((), ())), preferred_element_type=F32)
            cb2 = jnp.concatenate([cb, cb], axis=1)
            cb_parts += [cb2] * (GROUP_W // LANES)
            st = state_ref[g]
            yoff_parts.append(_dot(cg, st.astype(BF16)))
            bg_t = jnp.transpose(bm[:, g * SSD_STATE:(g + 1) * SSD_STATE]).astype(BF16)
            upd = _dot(bg_t, xw_b[:, g * GROUP_W:(g + 1) * GROUP_W])
            state_ref[g] = jnp.exp(a_end[:, g * GROUP_W:(g + 1) * GROUP_W]) * st + upd
        m_b = (jnp.concatenate(cb_parts, axis=1) * decay).astype(BF16)
        y_off = jnp.concatenate(yoff_parts, axis=1) * jnp.exp(cs)

        yd_parts = []
        for q in range(SSD_INNER // 256):
            sl = slice(q * 256, (q + 1) * 256)
            rhs = jnp.concatenate([xdt_b[:, sl]] * 4, axis=0) * bdmask
            yd_parts.append(_dot(m_b[:, sl], rhs))
        y = jnp.concatenate(yd_parts, axis=1) + y_off + xs * dexp

        yz = y * _silu(z_ref[rows, :].astype(F32))
        outs = []
        for g in range(SSD_GROUPS):
            yg = yz[:, g * GROUP_W:(g + 1) * GROUP_W]
            ms = jnp.mean(yg * yg, axis=-1, keepdims=True)
            outs.append(yg * lax.rsqrt(ms + EPS))
        o_ref[rows, :] = (jnp.concatenate(outs, axis=1) * nw).astype(o_ref.dtype)
        return carry

    lax.fori_loop(0, lb // CHUNK, chunk, 0)


def _ssd(proj, dt_raw, conv_w, conv_b, dtb, aexp, dexp, norm_w, consts, bsz, s_len, lb):
    t = proj.shape[0]
    nblk = s_len // lb
    xbc_blk0 = (6 * D_MODEL) // SSD_CONV_DIM
    full = lambda shape: pl.BlockSpec(shape, lambda b, c: (0,) * len(shape))
    e_mat, tril, dmask, trilmask, bdmask = consts
    return pl.pallas_call(
        functools.partial(_ssd_kernel, lb=lb),
        grid=(bsz, nblk),
        in_specs=[
            pl.BlockSpec((lb, SSD_CONV_DIM), lambda b, c: (b * nblk + c, xbc_blk0)),
            pl.BlockSpec((lb, SSD_INNER), lambda b, c: (b * nblk + c, 0)),
            pl.BlockSpec((lb, LANES), lambda b, c: (b * nblk + c, 0)),
            full((SSD_CONV, SSD_CONV_DIM)),
            full((1, SSD_CONV_DIM)),
            full((1, LANES)),
            full((1, SSD_INNER)),
            full((1, SSD_INNER)),
            full((1, SSD_INNER)),
            full(e_mat.shape), full(tril.shape), full(dmask.shape), full(trilmask.shape), full(bdmask.shape),
        ],
        out_specs=pl.BlockSpec((lb, SSD_INNER), lambda b, c: (b * nblk + c, 0)),
        out_shape=jax.ShapeDtypeStruct((t, SSD_INNER), BF16),
        scratch_shapes=[
            pltpu.VMEM((lb + 8, SSD_CONV_DIM), F32),
            pltpu.VMEM((lb, SSD_CONV_DIM), F32),
            pltpu.VMEM((lb, SSD_INNER), F32),
            pltpu.VMEM((SSD_GROUPS, SSD_STATE, GROUP_W), F32),
        ],
        compiler_params=_cparams(("parallel", "arbitrary")),
        name="ssd",
    )(proj, proj, dt_raw, conv_w, conv_b, dtb, aexp, dexp, norm_w, e_mat, tril, dmask, trilmask, bdmask)


def _qkprep_kernel(q_ref, k_ref, qw_ref, kw_ref, cos_ref, sa_ref, sb_ref, g_ref, pup_ref, pdn_ref, qo_ref, kto_ref):
    sw = 256
    cos = jnp.concatenate([cos_ref[...]] * (sw // LANES), axis=1)
    sa = jnp.concatenate([sa_ref[...]] * (sw // LANES), axis=1)
    sb = jnp.concatenate([sb_ref[...]] * (sw // LANES), axis=1)
    gmat = g_ref[...]
    pup = pup_ref[...]
    pdn = pdn_ref[...]

    def prep(x, w):
        ss = _dot((x * x).astype(BF16), gmat)
        xn = x * lax.rsqrt(ss * (1.0 / ATT_QK_DIM) + EPS) * w
        xb = xn.astype(BF16)
        return xn * cos + _dot(xb, pup) * sa + _dot(xb, pdn) * sb

    for c in range(ATT_WIDTH // sw):
        sl = slice(c * sw, (c + 1) * sw)
        q = prep(q_ref[:, sl].astype(F32), qw_ref[:, sl]) * (ATT_QK_DIM ** -0.5 * math.log2(math.e))
        qo_ref[:, sl] = q.astype(qo_ref.dtype)
        k = prep(k_ref[:, sl].astype(F32), kw_ref[:, sl])
        kto_ref[0, sl, :] = jnp.transpose(k).astype(kto_ref.dtype)


def _qk_prep(proj, qw, kw, cos_t, sa_t, sb_t, gmat, pup, pdn, bsz, s_len, tm):
    t = proj.shape[0]
    nblk = s_len // tm
    full = lambda shape: pl.BlockSpec(shape, lambda b, c: (0,) * len(shape))
    return pl.pallas_call(
        _qkprep_kernel,
        grid=(bsz, nblk),
        in_specs=[
            pl.BlockSpec((tm, ATT_WIDTH), lambda b, c: (b * nblk + c, 1)),
            pl.BlockSpec((tm, ATT_WIDTH), lambda b, c: (b * nblk + c, 2)),
            full((1, ATT_WIDTH)), full((1, ATT_WIDTH)),
            pl.BlockSpec((tm, LANES), lambda b, c: (c, 0)),
            pl.BlockSpec((tm, LANES), lambda b, c: (c, 0)),
            pl.BlockSpec((tm, LANES), lambda b, c: (c, 0)),
            full((256, 256)), full((256, 256)), full((256, 256)),
        ],
        out_specs=[
            pl.BlockSpec((tm, ATT_WIDTH), lambda b, c: (b * nblk + c, 0)),
            pl.BlockSpec((1, ATT_WIDTH, tm), lambda b, c: (b, 0, c)),
        ],
        out_shape=[jax.ShapeDtypeStruct((t, ATT_WIDTH), BF16),
                   jax.ShapeDtypeStruct((bsz, ATT_WIDTH, s_len), BF16)],
        compiler_params=_cparams(("parallel", "parallel")),
        name="qk_prep",
    )(proj, proj, qw, kw, cos_t, sa_t, sb_t, gmat, pup, pdn)


def _attn_kernel(q_ref, kt_ref, v_ref, lq1_ref, lk1_ref, lq2_ref, lk2_ref, sw_ref, o_ref,
                 m_ref, l_ref, acc_ref, *, tq, hp, lambda_init):
    i = pl.program_id(2)
    w = ATT_V_DIM
    lane = lax.broadcasted_iota(jnp.int32, (tq, w), 1)
    qqs = []
    for h in range(hp):
        q = q_ref[:, h * w:(h + 1) * w]
        zero = jnp.zeros_like(q)
        qqs.append(jnp.concatenate([jnp.where(lane < ATT_QK_DIM, q, zero),
                                    jnp.where(lane >= ATT_QK_DIM, q, zero)], axis=0))

    m_ref[...] = jnp.full(m_ref.shape, -jnp.inf, F32)
    l_ref[...] = jnp.zeros(l_ref.shape, F32)
    acc_ref[...] = jnp.zeros(acc_ref.shape, F32)

    def tile(j, mask):
        c0 = pl.multiple_of(j * tq, tq)
        for h in range(hp):
            s = _dot(qqs[h], kt_ref[0, h * w:(h + 1) * w, pl.ds(c0, tq)])
            if mask is not None:
                s = jnp.where(mask, s, -jnp.inf)
            m_prev = m_ref[h]
            m_new = jnp.maximum(m_prev, jnp.max(s, axis=1, keepdims=True))
            alpha = jnp.exp2(m_prev - m_new)
            p = jnp.exp2(s - jnp.concatenate([m_new] * (tq // LANES), axis=1))
            psum = p[:, 0:LANES]
            for c in range(1, tq // LANES):
                psum = psum + p[:, c * LANES:(c + 1) * LANES]
            l_ref[h] = alpha * l_ref[h] + psum
            acc_ref[h] = alpha * acc_ref[h] + _dot(p.astype(BF16), v_ref[pl.ds(c0, tq), h * w:(h + 1) * w])
            m_ref[h] = m_new

    def body(j, carry):
        tile(j, None)
        return carry

    lax.fori_loop(0, i, body, 0)

    row = lax.broadcasted_iota(jnp.int32, (2 * tq, tq), 0)
    col = lax.broadcasted_iota(jnp.int32, (2 * tq, tq), 1)
    row = jnp.where(row >= tq, row - tq, row)
    tile(i, (col // CHUNK) <= (row // CHUNK))

    lam = (jnp.exp(jnp.sum(lq1_ref[...] * lk1_ref[...], axis=1, keepdims=True))
           - jnp.exp(jnp.sum(lq2_ref[...] * lk2_ref[...], axis=1, keepdims=True)) + lambda_init)
    for h in range(hp):
        acc = acc_ref[h]
        l = jnp.sum(l_ref[h], axis=1, keepdims=True)
        o = acc[0:tq] / l[0:tq] - lam * (acc[tq:2 * tq] / l[tq:2 * tq])
        ms = jnp.mean(o * o, axis=-1, keepdims=True)
        o_ref[:, h * w:(h + 1) * w] = (o * lax.rsqrt(ms + EPS) * sw_ref[...] * (1.0 - lambda_init)).astype(o_ref.dtype)


def _attention(q_prep, k_t, proj, lq1, lk1, lq2, lk2, subln_w, lambda_init, bsz, s_len, tq, hp):
    t = q_prep.shape[0]
    nq = s_len // tq
    wblk = hp * ATT_V_DIM
    v_blk0 = (3 * D_MODEL) // wblk
    vec = lambda n: pl.BlockSpec((1, n), lambda b, h, i: (0, 0))
    return pl.pallas_call(
        functools.partial(_attn_kernel, tq=tq, hp=hp, lambda_init=lambda_init),
        grid=(bsz, ATT_HEADS // hp, nq),
        in_specs=[
            pl.BlockSpec((tq, wblk), lambda b, h, i: (b * nq + i, h)),
            pl.BlockSpec((1, wblk, s_len), lambda b, h, i: (b, h, 0)),
            pl.BlockSpec((s_len, wblk), lambda b, h, i: (b, v_blk0 + h)),
            vec(ATT_QK_DIM), vec(ATT_QK_DIM), vec(ATT_QK_DIM), vec(ATT_QK_DIM),
            vec(ATT_V_DIM),
        ],
        out_specs=pl.BlockSpec((tq, wblk), lambda b, h, i: (b * nq + i, h)),
        out_shape=jax.ShapeDtypeStruct((t, ATT_WIDTH), BF16),
        scratch_shapes=[
            pltpu.VMEM((hp, 2 * tq, LANES), F32),
            pltpu.VMEM((hp, 2 * tq, LANES), F32),
            pltpu.VMEM((hp, 2 * tq, ATT_V_DIM), F32),
        ],
        compiler_params=_cparams(("parallel", "parallel", "arbitrary")),
        name="attn",
    )(q_prep, k_t, proj, lq1, lk1, lq2, lk2, subln_w)


def _merge_kernel(ys_ref, ya_ref, gs_ref, ga_ref, x_ref, gb_ref, ws_ref, wa_ref, wo_ref, nw_ref, *rest,
                  with_router):
    if with_router:
        rw_ref, tril_ref, xo_ref, hn3_ref, w12_ref, pos_ref, routec_ref, cnt_ref, before_ref, run_ref = rest
    else:
        xo_ref, hn_ref = rest
    gs = _sigmoid(gs_ref[...].astype(F32) + gb_ref[0:1, :])
    ga = _sigmoid(ga_ref[...].astype(F32) + gb_ref[1:2, :])
    merged = gs * _dot(ys_ref[...], ws_ref[...]) + ga * _dot(ya_ref[...], wa_ref[...])
    xn = x_ref[...] + _dot(merged.astype(BF16), wo_ref[...])
    xo_ref[...] = xn
    ms = jnp.mean(xn * xn, axis=-1, keepdims=True)
    hn = xn * lax.rsqrt(ms + EPS) * nw_ref[...]
    if not with_router:
        hn_ref[...] = hn.astype(hn_ref.dtype)
        return

    @pl.when(pl.program_id(0) == 0)
    def _():
        run_ref[...] = jnp.zeros_like(run_ref)

    hn3_ref[...] = hn

    h1, h2, _ = _split3(hn)
    logits = _dot(h1, rw_ref[0]) + _dot(h1, rw_ref[1]) + _dot(h2, rw_ref[0])
    lane = lax.broadcasted_iota(jnp.int32, logits.shape, 1)
    neg = jnp.float32(-jnp.inf)
    lg = jnp.where(lane < N_EXPERTS, logits, neg)
    m1 = jnp.max(lg, axis=1, keepdims=True)
    i1 = jnp.min(jnp.where(lg == m1, lane, LANES), axis=1, keepdims=True)
    lg2 = jnp.where(lane == i1, neg, lg)
    m2 = jnp.max(lg2, axis=1, keepdims=True)
    i2 = jnp.min(jnp.where(lg2 == m2, lane, LANES), axis=1, keepdims=True)
    e2 = jnp.exp(m2 - m1)
    w1 = 1.0 / (1.0 + e2)
    w2 = e2 / (1.0 + e2)
    w12_ref[...] = jnp.where(lane == 0, w1, 0.0) + jnp.where(lane == 1, w2, 0.0)

    sel = jnp.where(lane == i1, 1.0, 0.0) + jnp.where(lane == i2, 1.0, 0.0)
    rank = _dot(tril_ref[...], sel.astype(BF16))
    before_ref[...] = jnp.broadcast_to(run_ref[...], before_ref.shape)
    r1 = jnp.sum(jnp.where(lane == i1, rank, 0.0), axis=1, keepdims=True)
    r2 = jnp.sum(jnp.where(lane == i2, rank, 0.0), axis=1, keepdims=True)
    route = (jnp.where(lane == 0, i1.astype(F32), 0.0) + jnp.where(lane == 1, i2.astype(F32), 0.0)
             + jnp.where(lane == 2, r1, 0.0) + jnp.where(lane == 3, r2, 0.0))
    pos_ref[...] = jnp.transpose(route)[0:8, :].astype(jnp.int32)
    routec_ref[...] = route
    run_ref[...] = run_ref[...] + jnp.sum(sel, axis=0, keepdims=True)
    cnt_ref[...] = run_ref[...]


def _merge(y_ssd, y_att, proj, x2d, gate_b, ws, wa, wo, nw, router_w, tril, tm):
    t = x2d.shape[0]
    with_router = router_w is not None
    row = lambda w, blk: pl.BlockSpec((tm, w), lambda i: (i, blk))
    full = lambda shape: pl.BlockSpec(shape, lambda i: (0,) * len(shape))
    in_specs = [
        row(D_MODEL, 0), row(D_MODEL, 0), row(D_MODEL, 4), row(D_MODEL, 5), row(D_MODEL, 0),
        full((2, D_MODEL)), full((D_MODEL, D_MODEL)), full((D_MODEL, D_MODEL)), full((D_MODEL, D_MODEL)),
        full((1, D_MODEL)),
    ]
    args = [y_ssd, y_att, proj, proj, x2d, gate_b, ws, wa, wo, nw]
    scratch = []
    if with_router:
        in_specs += [full((2, D_MODEL, LANES)), full((tm, tm))]
        args += [router_w, tril]
        out_specs = [
            row(D_MODEL, 0),
            row(D_MODEL, 0),
            row(LANES, 0),
            pl.BlockSpec((8, tm), lambda i: (0, i)),
            row(LANES, 0),
            full((1, LANES)),
            pl.BlockSpec((8, LANES), lambda i: (i, 0)),
        ]
        out_shape = [
            jax.ShapeDtypeStruct((t, D_MODEL), F32),
            jax.ShapeDtypeStruct((t, D_MODEL), F32),
            jax.ShapeDtypeStruct((t, LANES), F32),
            jax.ShapeDtypeStruct((8, t), jnp.int32),
            jax.ShapeDtypeStruct((t, LANES), F32),
            jax.ShapeDtypeStruct((1, LANES), F32),
            jax.ShapeDtypeStruct((8 * (t // tm), LANES), F32),
        ]
        scratch = [pltpu.VMEM((1, LANES), F32)]
    else:
        out_specs = [row(D_MODEL, 0), row(D_MODEL, 0)]
        out_shape = [jax.ShapeDtypeStruct((t, D_MODEL), F32), jax.ShapeDtypeStruct((t, D_MODEL), BF16)]
    return pl.pallas_call(
        functools.partial(_merge_kernel, with_router=with_router),
        grid=(t // tm,),
        in_specs=in_specs,
        out_specs=out_specs,
        out_shape=out_shape,
        scratch_shapes=scratch,
        compiler_params=_cparams(("arbitrary",) if with_router else ("parallel",)),
        name="merge_router" if with_router else "merge",
    )(*args)


def _ffn_kernel(hn_ref, x_ref, wg_ref, wu_ref, wd_ref, o_ref):
    f = pl.program_id(1)
    hn = hn_ref[...]
    h = (_silu(_dot(hn, wg_ref[...])) * _dot(hn, wu_ref[...])).astype(BF16)
    y = _dot(h, wd_ref[...])

    @pl.when(f == 0)
    def _():
        o_ref[...] = x_ref[...] + y

    @pl.when(f > 0)
    def _():
        o_ref[...] += y


def _ffn(hn, x2d, wg, wu, wd, tm, tf):
    t = x2d.shape[0]
    return pl.pallas_call(
        _ffn_kernel,
        grid=(t // tm, D_FF // tf),
        in_specs=[
            pl.BlockSpec((tm, D_MODEL), lambda i, f: (i, 0)),
            pl.BlockSpec((tm, D_MODEL), lambda i, f: (i, 0)),
            pl.BlockSpec((D_MODEL, tf), lambda i, f: (0, f)),
            pl.BlockSpec((D_MODEL, tf), lambda i, f: (0, f)),
            pl.BlockSpec((tf, D_MODEL), lambda i, f: (f, 0)),
        ],
        out_specs=pl.BlockSpec((tm, D_MODEL), lambda i, f: (i, 0)),
        out_shape=jax.ShapeDtypeStruct((t, D_MODEL), F32),
        compiler_params=_cparams(("parallel", "arbitrary")),
        name="ffn",
    )(hn, x2d, wg, wu, wd)


META_OFF, META_LAST, META_NVALID = 0, N_EXPERTS, 2 * N_EXPERTS
RUN_ALIGN = 8


def _dispatch_kernel(route_ref, np_ref, c_ref, d_ref, meta_ref, hn_ref, zeros_ref, xs_ref, ybuf_ref, sem, zsem,
                     *, tm, n_tiles):
    i = pl.program_id(0)

    @pl.when(i == 0)
    def _():
        def ztile(row0):
            return pltpu.make_async_copy(zeros_ref, xs_ref.at[pl.ds(pl.multiple_of(row0, MOE_TILE), MOE_TILE)], zsem)

        def each_zero_tile(fn):
            for e in range(N_EXPERTS):
                last = meta_ref[META_LAST + e]

                @pl.when(last >= 0)
                def _():
                    fn(ztile(last))
            for k in range(N_EXPERTS):
                g = meta_ref[META_NVALID] + k

                @pl.when(g < n_tiles)
                def _():
                    fn(ztile(g * MOE_TILE))

        each_zero_tile(lambda c: c.start())
        each_zero_tile(lambda c: c.wait())

    r = route_ref[...]
    e1, e2, p1, p2 = r[0:1, :], r[1:2, :], r[2:3, :], r[3:4, :]
    for e in range(N_EXPERTS):
        ce = c_ref[i * N_EXPERTS + e]
        p1 = p1 + jnp.where(e1 == e, ce, 0)
        p2 = p2 + jnp.where(e2 == e, ce, 0)
    kk = lax.broadcasted_iota(jnp.int32, (ybuf_ref.shape[0], tm), 0)
    sel = jnp.where(kk == p1, 1.0, jnp.where(kk == p2, 1.0, 0.0)).astype(BF16)
    ybuf_ref[...] = _dot(sel, hn_ref[...].astype(BF16))

    sizes = [1 << k for k in range(tm.bit_length() - 1, RUN_ALIGN.bit_length() - 2, -1)]

    def each_piece(fn):
        for e in range(N_EXPERTS):
            n = np_ref[i * N_EXPERTS + e]
            c = c_ref[i * N_EXPERTS + e]
            d = d_ref[i * N_EXPERTS + e]
            for sz in sizes:
                done = (n // (2 * sz)) * (2 * sz)

                @pl.when((n & sz) != 0)
                def _():
                    fn(pltpu.make_async_copy(ybuf_ref.at[pl.ds(pl.multiple_of(c + done, RUN_ALIGN), sz)],
                                             xs_ref.at[pl.ds(pl.multiple_of(d + done, RUN_ALIGN), sz)], sem))

    each_piece(lambda cp: cp.start())
    each_piece(lambda cp: cp.wait())


def _dispatch(route, tile_np, tile_c, tile_d, meta, hn, zeros_tile, tm, n_tiles):
    t = hn.shape[0]
    smem = pl.BlockSpec(memory_space=pltpu.SMEM)
    return pl.pallas_call(
        functools.partial(_dispatch_kernel, tm=tm, n_tiles=n_tiles),
        grid=(t // tm,),
        in_specs=[
            pl.BlockSpec((8, tm), lambda i: (0, i)),
            smem, smem, smem, smem,
            pl.BlockSpec((tm, D_MODEL), lambda i: (i, 0)),
            pl.BlockSpec(memory_space=pl.ANY),
        ],
        out_specs=pl.BlockSpec(memory_space=pl.ANY),
        out_shape=jax.ShapeDtypeStruct((n_tiles * MOE_TILE, D_MODEL), F32),
        scratch_shapes=[pltpu.VMEM((2 * tm + N_EXPERTS * RUN_ALIGN, D_MODEL), F32),
                        pltpu.SemaphoreType.DMA, pltpu.SemaphoreType.DMA],
        compiler_params=_cparams(("arbitrary",)),
        name="moe_dispatch",
    )(route, tile_np, tile_c, tile_d, meta, hn, zeros_tile)


def _gffn_kernel(te_ref, nv_ref, x3_ref, wg_ref, wu_ref, wd_ref, y3_ref, xb_ref, acc_ref):
    g = pl.program_id(0)
    f = pl.program_id(1)

    @pl.when((g >= nv_ref[0]) & (f == 0))
    def _():
        y3_ref[...] = jnp.zeros_like(y3_ref)

    @pl.when(g < nv_ref[0])
    def _():
        @pl.when(f == 0)
        def _():
            xb_ref[...] = x3_ref[...].astype(BF16)

        xb = xb_ref[...]
        h = (_silu(_dot(xb, wg_ref[0])) * _dot(xb, wu_ref[0])).astype(BF16)
        y = _dot(h, wd_ref[0])

        @pl.when(f == 0)
        def _():
            acc_ref[...] = y

        @pl.when(f == pl.num_programs(1) - 1)
        def _():
            y3_ref[...] = acc_ref[...] + y


def _grouped_ffn(tile_expert, n_valid, xs3, wg, wu, wd, tf):
    n_tiles = tile_expert.shape[0]
    assert D_FF // tf == 2
    grid_spec = pltpu.PrefetchScalarGridSpec(
        num_scalar_prefetch=2,
        grid=(n_tiles, D_FF // tf),
        in_specs=[
            pl.BlockSpec((MOE_TILE, D_MODEL), lambda g, f, te, nv: (g, 0)),
            pl.BlockSpec((1, D_MODEL, tf), lambda g, f, te, nv: (te[g], 0, f)),
            pl.BlockSpec((1, D_MODEL, tf), lambda g, f, te, nv: (te[g], 0, f)),
            pl.BlockSpec((1, tf, D_MODEL), lambda g, f, te, nv: (te[g], f, 0)),
        ],
        out_specs=pl.BlockSpec((MOE_TILE, D_MODEL), lambda g, f, te, nv: (g, 0)),
        scratch_shapes=[pltpu.VMEM((MOE_TILE, D_MODEL), BF16), pltpu.VMEM((MOE_TILE, D_MODEL), F32)],
    )
    return pl.pallas_call(
        _gffn_kernel,
        grid_spec=grid_spec,
        out_shape=jax.ShapeDtypeStruct(xs3.shape, F32),
        compiler_params=_cparams(("arbitrary", "arbitrary")),
        name="moe_ffn",
    )(tile_expert, n_valid, xs3, wg, wu, wd)


def _combine_kernel(routec_ref, w12_ref, x_ref, np_ref, c_ref, d_ref, y_ref, o_ref, ybuf_ref, sem, *, tm):
    i = pl.program_id(0)
    sizes = [1 << k for k in range(tm.bit_length() - 1, RUN_ALIGN.bit_length() - 2, -1)]

    def each_piece(fn):
        for e in range(N_EXPERTS):
            n = np_ref[i * N_EXPERTS + e]
            c = c_ref[i * N_EXPERTS + e]
            d = d_ref[i * N_EXPERTS + e]
            for sz in sizes:
                done = (n // (2 * sz)) * (2 * sz)

                @pl.when((n & sz) != 0)
                def _():
                    fn(pltpu.make_async_copy(y_ref.at[pl.ds(pl.multiple_of(d + done, RUN_ALIGN), sz)],
                                             ybuf_ref.at[pl.ds(pl.multiple_of(c + done, RUN_ALIGN), sz)], sem))

    @pl.when(i == 0)
    def _():
        ybuf_ref[...] = jnp.zeros_like(ybuf_ref)

    each_piece(lambda cp: cp.start())

    rc = routec_ref[...]
    e1, e2, p1, p2 = rc[:, 0:1], rc[:, 1:2], rc[:, 2:3], rc[:, 3:4]
    for e in range(N_EXPERTS):
        ce = c_ref[i * N_EXPERTS + e].astype(F32)
        p1 = p1 + jnp.where(e1 == float(e), ce, 0.0)
        p2 = p2 + jnp.where(e2 == float(e), ce, 0.0)
    w = w12_ref[...]
    kk = lax.broadcasted_iota(jnp.int32, (tm, ybuf_ref.shape[0]), 1).astype(F32)
    wm = jnp.where(kk == p1, w[:, 0:1], jnp.where(kk == p2, w[:, 1:2], 0.0))
    wm_hi = wm.astype(BF16)
    wm_lo = (wm - wm_hi.astype(F32)).astype(BF16)

    each_piece(lambda cp: cp.wait())
    yb = ybuf_ref[...].astype(BF16)
    o_ref[...] = x_ref[...] + _dot(wm_hi, yb) + _dot(wm_lo, yb)


def _combine(routec, w12, x2d, tile_np, tile_c, tile_d, y, tm):
    t = x2d.shape[0]
    smem = pl.BlockSpec(memory_space=pltpu.SMEM)
    return pl.pallas_call(
        functools.partial(_combine_kernel, tm=tm),
        grid=(t // tm,),
        in_specs=[
            pl.BlockSpec((tm, LANES), lambda i: (i, 0)),
            pl.BlockSpec((tm, LANES), lambda i: (i, 0)),
            pl.BlockSpec((tm, D_MODEL), lambda i: (i, 0)),
            smem, smem, smem,
            pl.BlockSpec(memory_space=pl.ANY),
        ],
        out_specs=pl.BlockSpec((tm, D_MODEL), lambda i: (i, 0)),
        out_shape=jax.ShapeDtypeStruct((t, D_MODEL), F32),
        scratch_shapes=[pltpu.VMEM((2 * tm + N_EXPERTS * RUN_ALIGN, D_MODEL), F32), pltpu.SemaphoreType.DMA],
        compiler_params=_cparams(("arbitrary",)),
        name="moe_combine",
    )(routec, w12, x2d, tile_np, tile_c, tile_d, y)


def _moe_tables(before_f, cnt_f, n_tiles):
    before = before_f[::8, :N_EXPERTS].astype(jnp.int32)
    total = cnt_f[:, :N_EXPERTS].astype(jnp.int32)
    n = jnp.concatenate([before[1:], total], axis=0) - before
    npad = (n + (RUN_ALIGN - 1)) // RUN_ALIGN * RUN_ALIGN
    c = jnp.cumsum(npad, axis=1) - npad
    rows = jnp.sum(npad, axis=0)
    per = (rows + (MOE_TILE - 1)) // MOE_TILE
    ends = jnp.cumsum(per)
    starts = ends - per
    n_valid = ends[-1:]
    g = jnp.minimum(jnp.arange(n_tiles, dtype=jnp.int32), n_valid - 1)
    tile_expert = jnp.sum((g[:, None] >= ends[None, :]).astype(jnp.int32), axis=1)
    last = jnp.where(per > 0, (ends - 1) * MOE_TILE, -1)
    meta = jnp.concatenate([starts * MOE_TILE, last, n_valid]).astype(jnp.int32)
    d = (starts * MOE_TILE)[None, :] + jnp.cumsum(npad, axis=0) - npad
    flat = lambda a: a.reshape(-1).astype(jnp.int32)
    return (tile_expert.astype(jnp.int32), n_valid.astype(jnp.int32), jnp.pad(meta, (0, LANES - meta.shape[0])),
            flat(npad), flat(c), flat(d))


def _ssd_consts():
    e_mat = np.zeros((LANES, SSD_INNER), np.float32)
    for h in range(SSD_HEADS):
        e_mat[h, h * SSD_HEAD_DIM:(h + 1) * SSD_HEAD_DIM] = 1.0
    tril = np.tril(np.ones((CHUNK, CHUNK), np.float32))
    dmask = np.tile(np.eye(CHUNK, dtype=np.float32), (1, SSD_HEADS))
    trilmask = np.tile(tril, (1, SSD_HEADS))
    blk = np.arange(256) // SSD_HEAD_DIM
    bdmask = (blk[:, None] == blk[None, :]).astype(np.float32)
    return (jnp.asarray(e_mat, BF16), jnp.asarray(tril, BF16), jnp.asarray(dmask, F32),
            jnp.asarray(trilmask, F32), jnp.asarray(bdmask, BF16))


def _rope_tables(s_len):
    half = ROPE_DIM // 2
    inv = ROPE_THETA ** (-jnp.arange(0, ROPE_DIM, 2, dtype=F32) / ROPE_DIM)
    ang = jnp.arange(s_len, dtype=F32)[:, None] * inv[None, :]
    cos, sin = jnp.cos(ang), jnp.sin(ang)
    pad = jnp.zeros((s_len, ATT_QK_DIM - ROPE_DIM), F32)
    cos_c = jnp.concatenate([cos, cos, pad + 1.0], axis=1)
    sa_c = jnp.concatenate([-sin, jnp.zeros_like(sin), pad], axis=1)
    sb_c = jnp.concatenate([jnp.zeros_like(sin), sin, pad], axis=1)
    two = lambda a: jnp.concatenate([a, a], axis=1)
    return two(cos_c), two(sa_c), two(sb_c)


def _split2_w(w):
    w = jnp.pad(w, ((0, 0), (0, LANES - w.shape[1])))
    hi = w.astype(BF16)
    lo = (w - hi.astype(F32)).astype(BF16)
    return jnp.stack([hi, lo])


def _pick(t, prefs):
    for p in prefs:
        if t % p == 0:
            return p
    return t


def kernel(x, norm_mix_w, w_in, conv_w, conv_b, dt_bias, a_log, d_skip, ssd_norm_w, q_norm_w, k_norm_w,
           lambda_q1, lambda_k1, lambda_q2, lambda_k2, subln_w, gate_b, w_br_ssd, w_br_att, w_out,
           norm_ffn_w, ffn_w_gate, ffn_w_up, ffn_w_down, router_w, moe_w_gate, moe_w_up, moe_w_down):
    bsz, s_len, d = x.shape
    depth = w_in.shape[0]
    t = bsz * s_len
    assert d == D_MODEL and s_len % 512 == 0

    tm_proj = _pick(t, (1024, 512))
    tm_row = _pick(t, (512,))
    lb = 256
    tm_qk = 512
    tq, heads_per_step = 512, 4

    consts = _ssd_consts()
    tril_tok = jnp.asarray(np.tril(np.ones((tm_row, tm_row), np.float32), -1), BF16)
    zeros_tile = jnp.zeros((MOE_TILE, D_MODEL), F32)
    cos_t, sa_t, sb_t = _rope_tables(s_len)
    blk64 = np.arange(256) // ATT_QK_DIM
    gmat = jnp.asarray((blk64[:, None] == blk64[None, :]).astype(np.float32), BF16)
    pup = jnp.asarray(np.eye(256, k=-(ROPE_DIM // 2), dtype=np.float32), BF16)
    pdn = jnp.asarray(np.eye(256, k=ROPE_DIM // 2, dtype=np.float32), BF16)

    o1 = SSD_INNER
    o2 = o1 + SSD_CONV_DIM
    o3 = o2 + SSD_HEADS
    o4 = o3 + ATT_WIDTH
    o5 = o4 + ATT_WIDTH
    o6 = o5 + ATT_WIDTH
    o7 = o6 + D_MODEL

    x2d = x.reshape(t, d)
    for i in range(depth):
        lambda_init = 0.8 - 0.6 * math.exp(-0.3 * i)
        wi = w_in[i]
        w_main = jnp.concatenate([wi[:, :o1], wi[:, o3:], wi[:, o1:o2]], axis=1).astype(BF16)
        w_dt = _split2_w(wi[:, o2:o3])
        proj, dt_raw = _in_proj(x2d, norm_mix_w[i][None, :], w_main, w_dt, tm_proj, SSD_CONV_DIM)

        dtb = jnp.pad(dt_bias[i], (0, LANES - SSD_HEADS))[None, :]
        aexp = jnp.repeat(-jnp.exp(a_log[i].astype(F32)), SSD_HEAD_DIM)[None, :]
        dexp = jnp.repeat(d_skip[i], SSD_HEAD_DIM)[None, :]
        y_ssd = _ssd(proj, dt_raw, conv_w[i], conv_b[i][None, :], dtb, aexp, dexp, ssd_norm_w[i][None, :],
                     consts, bsz, s_len, lb)

        qw = jnp.tile(q_norm_w[i], ATT_WIDTH // ATT_QK_DIM)[None, :]
        kw = jnp.tile(k_norm_w[i], ATT_WIDTH // ATT_QK_DIM)[None, :]
        q_prep, k_t = _qk_prep(proj, qw, kw, cos_t, sa_t, sb_t, gmat, pup, pdn, bsz, s_len, tm_qk)
        y_att = _attention(q_prep, k_t, proj, lambda_q1[i][None, :], lambda_k1[i][None, :],
                           lambda_q2[i][None, :], lambda_k2[i][None, :], subln_w[i][None, :],
                           lambda_init, bsz, s_len, tq, heads_per_step)

        j = i // 2
        is_moe = i % 2 == 1
        rw = _split2_w(router_w[j]) if is_moe else None
        outs = _merge(y_ssd, y_att, proj, x2d, gate_b[i], w_br_ssd[i].astype(BF16), w_br_att[i].astype(BF16),
                      w_out[i].astype(BF16), norm_ffn_w[i][None, :], rw, tril_tok, tm_row)
        if is_moe:
            x2d, hn3, w12, route, routec, cnt_f, before_f = outs
            pad_rows = (t // tm_row) * N_EXPERTS * (RUN_ALIGN - 1)
            n_tiles = -(-(2 * t + pad_rows) // MOE_TILE) + N_EXPERTS
            tile_e, n_valid, meta, tile_np, tile_c, tile_d = _moe_tables(before_f, cnt_f, n_tiles)
            xs3 = _dispatch(route, tile_np, tile_c, tile_d, meta, hn3, zeros_tile, tm_row, n_tiles)
            y3 = _grouped_ffn(tile_e, n_valid, xs3, moe_w_gate[j].astype(BF16),
                              moe_w_up[j].astype(BF16), moe_w_down[j].astype(BF16), D_FF // 2)
            x2d = _combine(routec, w12, x2d, tile_np, tile_c, tile_d, y3, tm_row)
        else:
            x2d, hn = outs
            x2d = _ffn(hn, x2d, ffn_w_gate[j].astype(BF16), ffn_w_up[j].astype(BF16),
                       ffn_w_down[j].astype(BF16), tm_row, D_FF // 2)
    return x2d.reshape(bsz, s_len, d)
```

```python
import functools
import math

import numpy as np
import jax
import jax.numpy as jnp
from jax import lax
from jax.experimental import pallas as pl
from jax.experimental.pallas import tpu as pltpu

F32 = jnp.float32
BF16 = jnp.bfloat16

D_MODEL = 1024
CHUNK = 64
EPS = 1e-6
SSD_HEADS = 16
SSD_HEAD_DIM = 64
SSD_INNER = SSD_HEADS * SSD_HEAD_DIM
SSD_GROUPS = 2
SSD_STATE = 128
SSD_CONV = 4
SSD_CONV_DIM = SSD_INNER + 2 * SSD_GROUPS * SSD_STATE
ATT_HEADS = 8
ATT_QK_DIM = 64
ATT_V_DIM = 128
ATT_WIDTH = ATT_HEADS * ATT_V_DIM
ROPE_THETA = 500000.0
ROPE_DIM = ATT_QK_DIM // 4
D_FF = 2816
N_EXPERTS = 8
MOE_TILE = 512
LANES = 128
GROUP_W = SSD_INNER // SSD_GROUPS
MAIN_COLS = 6 * D_MODEL + SSD_CONV_DIM
VMEM_LIMIT = 56 * 1024 * 1024


def _cparams(sem):
    return pltpu.CompilerParams(dimension_semantics=sem, vmem_limit_bytes=VMEM_LIMIT)


def _split3(a):
    h1 = a.astype(BF16)
    r1 = a - h1.astype(F32)
    h2 = r1.astype(BF16)
    r2 = r1 - h2.astype(F32)
    return h1, h2, r2.astype(BF16)


def _dot(a, b):
    return jnp.dot(a, b, preferred_element_type=F32)


def _sigmoid(x):
    return 1.0 / (1.0 + jnp.exp(-x))


def _silu(x):
    return x * _sigmoid(x)


def _inproj_kernel(x_ref, nw_ref, w_ref, wdt_ref, o_ref, dt_ref, xn_ref):
    @pl.when(pl.program_id(1) == 0)
    def _():
        x = x_ref[...]
        ms = jnp.mean(x * x, axis=-1, keepdims=True)
        xn = x * lax.rsqrt(ms + EPS) * nw_ref[...]
        x1, x2, _ = _split3(xn)
        xn_ref[...] = x1
        dt_ref[...] = _dot(x1, wdt_ref[0]) + _dot(x1, wdt_ref[1]) + _dot(x2, wdt_ref[0])

    o_ref[...] = _dot(xn_ref[...], w_ref[...]).astype(o_ref.dtype)


def _in_proj(x2d, norm_w, w_main, w_dt, tm, tn):
    t, d = x2d.shape
    n = w_main.shape[1]
    return pl.pallas_call(
        _inproj_kernel,
        grid=(t // tm, n // tn),
        in_specs=[
            pl.BlockSpec((tm, d), lambda i, j: (i, 0)),
            pl.BlockSpec((1, d), lambda i, j: (0, 0)),
            pl.BlockSpec((d, tn), lambda i, j: (0, j)),
            pl.BlockSpec((2, d, LANES), lambda i, j: (0, 0, 0)),
        ],
        out_specs=[
            pl.BlockSpec((tm, tn), lambda i, j: (i, j)),
            pl.BlockSpec((tm, LANES), lambda i, j: (i, 0)),
        ],
        out_shape=[jax.ShapeDtypeStruct((t, n), BF16), jax.ShapeDtypeStruct((t, LANES), F32)],
        scratch_shapes=[pltpu.VMEM((tm, d), BF16)],
        compiler_params=_cparams(("parallel", "arbitrary")),
        name="in_proj",
    )(x2d, norm_w, w_main, w_dt)


def _ssd_kernel(xbc_ref, z_ref, dt_ref, convw_ref, convb_ref, dtb_ref, aexp_ref, dexp_ref, nw_ref,
                e_ref, tril_ref, dmask_ref, trilmask_ref, bdmask_ref,
                o_ref, xpad_ref, u_ref, dte_ref, state_ref, *, lb):
    @pl.when(pl.program_id(1) == 0)
    def _():
        xpad_ref[0:8, :] = jnp.zeros((8, SSD_CONV_DIM), F32)
        state_ref[...] = jnp.zeros_like(state_ref)

    xpad_ref[8:8 + lb, :] = xbc_ref[...].astype(F32)
    rp = 128
    for c in range(SSD_CONV_DIM // LANES):
        sl = slice(c * LANES, (c + 1) * LANES)
        for r in range(lb // rp):
            base = 8 - (SSD_CONV - 1) + r * rp
            acc = convb_ref[:, sl] + convw_ref[0:1, sl] * xpad_ref[pl.ds(base, rp), sl]
            for k in range(1, SSD_CONV):
                acc = acc + convw_ref[k:k + 1, sl] * xpad_ref[pl.ds(base + k, rp), sl]
            u_ref[r * rp:(r + 1) * rp, sl] = _silu(acc)
    xpad_ref[0:8, :] = xpad_ref[lb:lb + 8, :]

    t = dt_ref[...] + dtb_ref[...]
    dt = jnp.maximum(t, 0.0) + jnp.log(1.0 + jnp.exp(-jnp.abs(t)))
    d1, d2, d3 = _split3(dt)
    e = e_ref[...]
    dte_ref[...] = _dot(d1, e) + _dot(d2, e) + _dot(d3, e)

    tril = tril_ref[...]
    dmask = dmask_ref[...]
    trilmask = trilmask_ref[...] > 0.5
    bdmask = bdmask_ref[...]
    aexp = aexp_ref[...]
    dexp = dexp_ref[...]
    nw = nw_ref[...]

    def chunk(j, carry):
        r0 = pl.multiple_of(j * CHUNK, CHUNK)
        rows = pl.ds(r0, CHUNK)
        xs = u_ref[rows, 0:SSD_INNER]
        bm = u_ref[rows, SSD_INNER:SSD_INNER + SSD_GROUPS * SSD_STATE]
        cm = u_ref[rows, SSD_INNER + SSD_GROUPS * SSD_STATE:SSD_CONV_DIM]
        dte = dte_ref[rows, :]
        a1, a2, a3 = _split3(dte * aexp)
        cs = _dot(tril, a1) + _dot(tril, a2) + _dot(tril, a3)
        rowv = jnp.sum(cs * dmask, axis=0, keepdims=True)
        decay = jnp.exp(jnp.where(trilmask, cs - rowv, -1e30))
        a_end = cs[CHUNK - 1:CHUNK, :]
        to_end = jnp.exp(a_end - cs)
        xdt = xs * dte
        xdt_b = xdt.astype(BF16)
        xw_b = (xdt * to_end).astype(BF16)
        bm_b = bm.astype(BF16)
        cm_b = cm.astype(BF16)

        cb_parts = []
        yoff_parts = []
        for g in range(SSD_GROUPS):
            bg = bm_b[:, g * SSD_STATE:(g + 1) * SSD_STATE]
            cg = cm_b[:, g * SSD_STATE:(g + 1) * SSD_STATE]
            cb = lax.dot_general(cg, bg, (((1,), (1,)), ((), ())), preferred_element_type=F32)
            cb2 = jnp.concatenate([cb, cb], axis=1)
            cb_parts += [cb2] * (GROUP_W // LANES)
            st = state_ref[g]
            yoff_parts.append(_dot(cg, st.astype(BF16)))
            bg_t = jnp.transpose(bm[:, g * SSD_STATE:(g + 1) * SSD_STATE]).astype(BF16)
            upd = _dot(bg_t, xw_b[:, g * GROUP_W:(g + 1) * GROUP_W])
            state_ref[g] = jnp.exp(a_end[:, g * GROUP_W:(g + 1) * GROUP_W]) * st + upd
        m_b = (jnp.concatenate(cb_parts, axis=1) * decay).astype(BF16)
        y_off = jnp.concatenate(yoff_parts, axis=1) * jnp.exp(cs)

        yd_parts = []
        for q in range(SSD_INNER // 256):
            sl = slice(q * 256, (q + 1) * 256)
            rhs = jnp.concatenate([xdt_b[:, sl]] * 4, axis=0) * bdmask
            yd_parts.append(_dot(m_b[:, sl], rhs))
        y = jnp.concatenate(yd_parts, axis=1) + y_off + xs * dexp

        yz = y * _silu(z_ref[rows, :].astype(F32))
        outs = []
        for g in range(SSD_GROUPS):
            yg = yz[:, g * GROUP_W:(g + 1) * GROUP_W]
            ms = jnp.mean(yg * yg, axis=-1, keepdims=True)
            outs.append(yg * lax.rsqrt(ms + EPS))
        o_ref[rows, :] = (jnp.concatenate(outs, axis=1) * nw).astype(o_ref.dtype)
        return carry

    lax.fori_loop(0, lb // CHUNK, chunk, 0)


def _ssd(proj, dt_raw, conv_w, conv_b, dtb, aexp, dexp, norm_w, consts, bsz, s_len, lb):
    t = proj.shape[0]
    nblk = s_len // lb
    xbc_blk0 = (6 * D_MODEL) // SSD_CONV_DIM
    full = lambda shape: pl.BlockSpec(shape, lambda b, c: (0,) * len(shape))
    e_mat, tril, dmask, trilmask, bdmask = consts
    return pl.pallas_call(
        functools.partial(_ssd_kernel, lb=lb),
        grid=(bsz, nblk),
        in_specs=[
            pl.BlockSpec((lb, SSD_CONV_DIM), lambda b, c: (b * nblk + c, xbc_blk0)),
            pl.BlockSpec((lb, SSD_INNER), lambda b, c: (b * nblk + c, 0)),
            pl.BlockSpec((lb, LANES), lambda b, c: (b * nblk + c, 0)),
            full((SSD_CONV, SSD_CONV_DIM)),
            full((1, SSD_CONV_DIM)),
            full((1, LANES)),
            full((1, SSD_INNER)),
            full((1, SSD_INNER)),
            full((1, SSD_INNER)),
            full(e_mat.shape), full(tril.shape), full(dmask.shape), full(trilmask.shape), full(bdmask.shape),
        ],
        out_specs=pl.BlockSpec((lb, SSD_INNER), lambda b, c: (b * nblk + c, 0)),
        out_shape=jax.ShapeDtypeStruct((t, SSD_INNER), BF16),
        scratch_shapes=[
            pltpu.VMEM((lb + 8, SSD_CONV_DIM), F32),
            pltpu.VMEM((lb, SSD_CONV_DIM), F32),
            pltpu.VMEM((lb, SSD_INNER), F32),
            pltpu.VMEM((SSD_GROUPS, SSD_STATE, GROUP_W), F32),
        ],
        compiler_params=_cparams(("parallel", "arbitrary")),
        name="ssd",
    )(proj, proj, dt_raw, conv_w, conv_b, dtb, aexp, dexp, norm_w, e_mat, tril, dmask, trilmask, bdmask)


def _qkprep_kernel(q_ref, k_ref, qw_ref, kw_ref, cos_ref, sa_ref, sb_ref, g_ref, pup_ref, pdn_ref, qo_ref, kto_ref):
    sw = 256
    cos = jnp.concatenate([cos_ref[...]] * (sw // LANES), axis=1)
    sa = jnp.concatenate([sa_ref[...]] * (sw // LANES), axis=1)
    sb = jnp.concatenate([sb_ref[...]] * (sw // LANES), axis=1)
    gmat = g_ref[...]
    pup = pup_ref[...]
    pdn = pdn_ref[...]

    def prep(x, w):
        ss = _dot((x * x).astype(BF16), gmat)
        xn = x * lax.rsqrt(ss * (1.0 / ATT_QK_DIM) + EPS) * w
        xb = xn.astype(BF16)
        return xn * cos + _dot(xb, pup) * sa + _dot(xb, pdn) * sb

    for c in range(ATT_WIDTH // sw):
        sl = slice(c * sw, (c + 1) * sw)
        q = prep(q_ref[:, sl].astype(F32), qw_ref[:, sl]) * (ATT_QK_DIM ** -0.5 * math.log2(math.e))
        qo_ref[:, sl] = q.astype(qo_ref.dtype)
        k = prep(k_ref[:, sl].astype(F32), kw_ref[:, sl])
        kto_ref[0, sl, :] = jnp.transpose(k).astype(kto_ref.dtype)


def _qk_prep(proj, qw, kw, cos_t, sa_t, sb_t, gmat, pup, pdn, bsz, s_len, tm):
    t = proj.shape[0]
    nblk = s_len // tm
    full = lambda shape: pl.BlockSpec(shape, lambda b, c: (0,) * len(shape))
    return pl.pallas_call(
        _qkprep_kernel,
        grid=(bsz, nblk),
        in_specs=[
            pl.BlockSpec((tm, ATT_WIDTH), lambda b, c: (b * nblk + c, 1)),
            pl.BlockSpec((tm, ATT_WIDTH), lambda b, c: (b * nblk + c, 2)),
            full((1, ATT_WIDTH)), full((1, ATT_WIDTH)),
            pl.BlockSpec((tm, LANES), lambda b, c: (c, 0)),
            pl.BlockSpec((tm, LANES), lambda b, c: (c, 0)),
            pl.BlockSpec((tm, LANES), lambda b, c: (c, 0)),
            full((256, 256)), full((256, 256)), full((256, 256)),
        ],
        out_specs=[
            pl.BlockSpec((tm, ATT_WIDTH), lambda b, c: (b * nblk + c, 0)),
            pl.BlockSpec((1, ATT_WIDTH, tm), lambda b, c: (b, 0, c)),
        ],
        out_shape=[jax.ShapeDtypeStruct((t, ATT_WIDTH), BF16),
                   jax.ShapeDtypeStruct((bsz, ATT_WIDTH, s_len), BF16)],
        compiler_params=_cparams(("parallel", "parallel")),
        name="qk_prep",
    )(proj, proj, qw, kw, cos_t, sa_t, sb_t, gmat, pup, pdn)


def _attn_kernel(q_ref, kt_ref, v_ref, lq1_ref, lk1_ref, lq2_ref, lk2_ref, sw_ref, o_ref,
                 m_ref, l_ref, acc_ref, *, tq, hp, lambda_init):
    i = pl.program_id(2)
    w = ATT_V_DIM
    lane = lax.broadcasted_iota(jnp.int32, (tq, w), 1)
    qqs = []
    for h in range(hp):
        q = q_ref[:, h * w:(h + 1) * w]
        zero = jnp.zeros_like(q)
        qqs.append(jnp.concatenate([jnp.where(lane < ATT_QK_DIM, q, zero),
                                    jnp.where(lane >= ATT_QK_DIM, q, zero)], axis=0))

    m_ref[...] = jnp.full(m_ref.shape, -jnp.inf, F32)
    l_ref[...] = jnp.zeros(l_ref.shape, F32)
    acc_ref[...] = jnp.zeros(acc_ref.shape, F32)

    def tile(j, mask):
        c0 = pl.multiple_of(j * tq, tq)
        for h in range(hp):
            s = _dot(qqs[h], kt_ref[0, h * w:(h + 1) * w, pl.ds(c0, tq)])
            if mask is not None:
                s = jnp.where(mask, s, -jnp.inf)
            m_prev = m_ref[h]
            m_new = jnp.maximum(m_prev, jnp.max(s, axis=1, keepdims=True))
            alpha = jnp.exp2(m_prev - m_new)
            p = jnp.exp2(s - jnp.concatenate([m_new] * (tq // LANES), axis=1))
            psum = p[:, 0:LANES]
            for c in range(1, tq // LANES):
                psum = psum + p[:, c * LANES:(c + 1) * LANES]
            l_ref[h] = alpha * l_ref[h] + psum
            acc_ref[h] = alpha * acc_ref[h] + _dot(p.astype(BF16), v_ref[pl.ds(c0, tq), h * w:(h + 1) * w])
            m_ref[h] = m_new

    def body(j, carry):
        tile(j, None)
        return carry

    lax.fori_loop(0, i, body, 0)

    row = lax.broadcasted_iota(jnp.int32, (2 * tq, tq), 0)
    col = lax.broadcasted_iota(jnp.int32, (2 * tq, tq), 1)
    row = jnp.where(row >= tq, row - tq, row)
    tile(i, (col // CHUNK) <= (row // CHUNK))

    lam = (jnp.exp(jnp.sum(lq1_ref[...] * lk1_ref[...], axis=1, keepdims=True))
           - jnp.exp(jnp.sum(lq2_ref[...] * lk2_ref[...], axis=1, keepdims=True)) + lambda_init)
    for h in range(hp):
        acc = acc_ref[h]
        l = jnp.sum(l_ref[h], axis=1, keepdims=True)
        o = acc[0:tq] / l[0:tq] - lam * (acc[tq:2 * tq] / l[tq:2 * tq])
        ms = jnp.mean(o * o, axis=-1, keepdims=True)
        o_ref[:, h * w:(h + 1) * w] = (o * lax.rsqrt(ms + EPS) * sw_ref[...] * (1.0 - lambda_init)).astype(o_ref.dtype)


def _attention(q_prep, k_t, proj, lq1, lk1, lq2, lk2, subln_w, lambda_init, bsz, s_len, tq, hp):
    t = q_prep.shape[0]
    nq = s_len // tq
    wblk = hp * ATT_V_DIM
    v_blk0 = (3 * D_MODEL) // wblk
    vec = lambda n: pl.BlockSpec((1, n), lambda b, h, i: (0, 0))
    return pl.pallas_call(
        functools.partial(_attn_kernel, tq=tq, hp=hp, lambda_init=lambda_init),
        grid=(bsz, ATT_HEADS // hp, nq),
        in_specs=[
            pl.BlockSpec((tq, wblk), lambda b, h, i: (b * nq + i, h)),
            pl.BlockSpec((1, wblk, s_len), lambda b, h, i: (b, h, 0)),
            pl.BlockSpec((s_len, wblk), lambda b, h, i: (b, v_blk0 + h)),
            vec(ATT_QK_DIM), vec(ATT_QK_DIM), vec(ATT_QK_DIM), vec(ATT_QK_DIM),
            vec(ATT_V_DIM),
        ],
        out_specs=pl.BlockSpec((tq, wblk), lambda b, h, i: (b * nq + i, h)),
        out_shape=jax.ShapeDtypeStruct((t, ATT_WIDTH), BF16),
        scratch_shapes=[
            pltpu.VMEM((hp, 2 * tq, LANES), F32),
            pltpu.VMEM((hp, 2 * tq, LANES), F32),
            pltpu.VMEM((hp, 2 * tq, ATT_V_DIM), F32),
        ],
        compiler_params=_cparams(("parallel", "parallel", "arbitrary")),
        name="attn",
    )(q_prep, k_t, proj, lq1, lk1, lq2, lk2, subln_w)


def _merge_kernel(ys_ref, ya_ref, gs_ref, ga_ref, x_ref, gb_ref, ws_ref, wa_ref, wo_ref, nw_ref, *rest,
                  with_router):
    if with_router:
        rw_ref, tril_ref, xo_ref, hn3_ref, w12_ref, pos_ref, routec_ref, cnt_ref, before_ref, run_ref = rest
    else:
        xo_ref, hn_ref = rest
    gs = _sigmoid(gs_ref[...].astype(F32) + gb_ref[0:1, :])
    ga = _sigmoid(ga_ref[...].astype(F32) + gb_ref[1:2, :])
    merged = gs * _dot(ys_ref[...], ws_ref[...]) + ga * _dot(ya_ref[...], wa_ref[...])
    xn = x_ref[...] + _dot(merged.astype(BF16), wo_ref[...])
    xo_ref[...] = xn
    ms = jnp.mean(xn * xn, axis=-1, keepdims=True)
    hn = xn * lax.rsqrt(ms + EPS) * nw_ref[...]
    if not with_router:
        hn_ref[...] = hn.astype(hn_ref.dtype)
        return

    @pl.when(pl.program_id(0) == 0)
    def _():
        run_ref[...] = jnp.zeros_like(run_ref)

    hn3_ref[...] = hn

    h1, h2, _ = _split3(hn)
    logits = _dot(h1, rw_ref[0]) + _dot(h1, rw_ref[1]) + _dot(h2, rw_ref[0])
    lane = lax.broadcasted_iota(jnp.int32, logits.shape, 1)
    neg = jnp.float32(-jnp.inf)
    lg = jnp.where(lane < N_EXPERTS, logits, neg)
    m1 = jnp.max(lg, axis=1, keepdims=True)
    i1 = jnp.min(jnp.where(lg == m1, lane, LANES), axis=1, keepdims=True)
    lg2 = jnp.where(lane == i1, neg, lg)
    m2 = jnp.max(lg2, axis=1, keepdims=True)
    i2 = jnp.min(jnp.where(lg2 == m2, lane, LANES), axis=1, keepdims=True)
    e2 = jnp.exp(m2 - m1)
    w1 = 1.0 / (1.0 + e2)
    w2 = e2 / (1.0 + e2)
    w12_ref[...] = jnp.where(lane == 0, w1, 0.0) + jnp.where(lane == 1, w2, 0.0)

    sel = jnp.where(lane == i1, 1.0, 0.0) + jnp.where(lane == i2, 1.0, 0.0)
    rank = _dot(tril_ref[...], sel.astype(BF16))
    before_ref[...] = jnp.broadcast_to(run_ref[...], before_ref.shape)
    r1 = jnp.sum(jnp.where(lane == i1, rank, 0.0), axis=1, keepdims=True)
    r2 = jnp.sum(jnp.where(lane == i2, rank, 0.0), axis=1, keepdims=True)
    route = (jnp.where(lane == 0, i1.astype(F32), 0.0) + jnp.where(lane == 1, i2.astype(F32), 0.0)
             + jnp.where(lane == 2, r1, 0.0) + jnp.where(lane == 3, r2, 0.0))
    pos_ref[...] = jnp.transpose(route)[0:8, :].astype(jnp.int32)
    routec_ref[...] = route
    run_ref[...] = run_ref[...] + jnp.sum(sel, axis=0, keepdims=True)
    cnt_ref[...] = run_ref[...]


def _merge(y_ssd, y_att, proj, x2d, gate_b, ws, wa, wo, nw, router_w, tril, tm):
    t = x2d.shape[0]
    with_router = router_w is not None
    row = lambda w, blk: pl.BlockSpec((tm, w), lambda i: (i, blk))
    full = lambda shape: pl.BlockSpec(shape, lambda i: (0,) * len(shape))
    in_specs = [
        row(D_MODEL, 0), row(D_MODEL, 0), row(D_MODEL, 4), row(D_MODEL, 5), row(D_MODEL, 0),
        full((2, D_MODEL)), full((D_MODEL, D_MODEL)), full((D_MODEL, D_MODEL)), full((D_MODEL, D_MODEL)),
        full((1, D_MODEL)),
    ]
    args = [y_ssd, y_att, proj, proj, x2d, gate_b, ws, wa, wo, nw]
    scratch = []
    if with_router:
        in_specs += [full((2, D_MODEL, LANES)), full((tm, tm))]
        args += [router_w, tril]
        out_specs = [
            row(D_MODEL, 0),
            row(D_MODEL, 0),
            row(LANES, 0),
            pl.BlockSpec((8, tm), lambda i: (0, i)),
            row(LANES, 0),
            full((1, LANES)),
            pl.BlockSpec((8, LANES), lambda i: (i, 0)),
        ]
        out_shape = [
            jax.ShapeDtypeStruct((t, D_MODEL), F32),
            jax.ShapeDtypeStruct((t, D_MODEL), F32),
            jax.ShapeDtypeStruct((t, LANES), F32),
            jax.ShapeDtypeStruct((8, t), jnp.int32),
            jax.ShapeDtypeStruct((t, LANES), F32),
            jax.ShapeDtypeStruct((1, LANES), F32),
            jax.ShapeDtypeStruct((8 * (t // tm), LANES), F32),
        ]
        scratch = [pltpu.VMEM((1, LANES), F32)]
    else:
        out_specs = [row(D_MODEL, 0), row(D_MODEL, 0)]
        out_shape = [jax.ShapeDtypeStruct((t, D_MODEL), F32), jax.ShapeDtypeStruct((t, D_MODEL), BF16)]
    return pl.pallas_call(
        functools.partial(_merge_kernel, with_router=with_router),
        grid=(t // tm,),
        in_specs=in_specs,
        out_specs=out_specs,
        out_shape=out_shape,
        scratch_shapes=scratch,
        compiler_params=_cparams(("arbitrary",) if with_router else ("parallel",)),
        name="merge_router" if with_router else "merge",
    )(*args)


def _ffn_kernel(hn_ref, x_ref, wg_ref, wu_ref, wd_ref, o_ref):
    f = pl.program_id(1)
    hn = hn_ref[...]
    h = (_silu(_dot(hn, wg_ref[...])) * _dot(hn, wu_ref[...])).astype(BF16)
    y = _dot(h, wd_ref[...])

    @pl.when(f == 0)
    def _():
        o_ref[...] = x_ref[...] + y

    @pl.when(f > 0)
    def _():
        o_ref[...] += y


def _ffn(hn, x2d, wg, wu, wd, tm, tf):
    t = x2d.shape[0]
    return pl.pallas_call(
        _ffn_kernel,
        grid=(t // tm, D_FF // tf),
        in_specs=[
            pl.BlockSpec((tm, D_MODEL), lambda i, f: (i, 0)),
            pl.BlockSpec((tm, D_MODEL), lambda i, f: (i, 0)),
            pl.BlockSpec((D_MODEL, tf), lambda i, f: (0, f)),
            pl.BlockSpec((D_MODEL, tf), lambda i, f: (0, f)),
            pl.BlockSpec((tf, D_MODEL), lambda i, f: (f, 0)),
        ],
        out_specs=pl.BlockSpec((tm, D_MODEL), lambda i, f: (i, 0)),
        out_shape=jax.ShapeDtypeStruct((t, D_MODEL), F32),
        compiler_params=_cparams(("parallel", "arbitrary")),
        name="ffn",
    )(hn, x2d, wg, wu, wd)


META_OFF, META_LAST, META_NVALID = 0, N_EXPERTS, 2 * N_EXPERTS
RUN_ALIGN = 8


def _dispatch_kernel(route_ref, np_ref, c_ref, d_ref, meta_ref, hn_ref, xs_ref, ybuf_ref, zeros_ref, sem, zsem,
                     *, tm, n_tiles):
    i = pl.program_id(0)

    @pl.when(i == 0)
    def _():
        zeros_ref[...] = jnp.zeros_like(zeros_ref)

        def ztile(row0):
            return pltpu.make_async_copy(zeros_ref, xs_ref.at[pl.ds(pl.multiple_of(row0, MOE_TILE), MOE_TILE)], zsem)

        def each_zero_tile(fn):
            for e in range(N_EXPERTS):
                last = meta_ref[META_LAST + e]

                @pl.when(last >= 0)
                def _():
                    fn(ztile(last))
            for k in range(N_EXPERTS):
                g = meta_ref[META_NVALID] + k

                @pl.when(g < n_tiles)
                def _():
                    fn(ztile(g * MOE_TILE))

        each_zero_tile(lambda c: c.start())
        each_zero_tile(lambda c: c.wait())

    r = route_ref[...]
    e1, e2, p1, p2 = r[0:1, :], r[1:2, :], r[2:3, :], r[3:4, :]
    for e in range(N_EXPERTS):
        ce = c_ref[i * N_EXPERTS + e]
        p1 = p1 + jnp.where(e1 == e, ce, 0)
        p2 = p2 + jnp.where(e2 == e, ce, 0)
    kk = lax.broadcasted_iota(jnp.int32, (ybuf_ref.shape[0], tm), 0)
    sel = jnp.where(kk == p1, 1.0, jnp.where(kk == p2, 1.0, 0.0)).astype(BF16)
    ybuf_ref[...] = _dot(sel, hn_ref[...].astype(BF16))

    sizes = [1 << k for k in range(tm.bit_length() - 1, RUN_ALIGN.bit_length() - 2, -1)]

    def each_piece(fn):
        for e in range(N_EXPERTS):
            n = np_ref[i * N_EXPERTS + e]
            c = c_ref[i * N_EXPERTS + e]
            d = d_ref[i * N_EXPERTS + e]
            for sz in sizes:
                done = (n // (2 * sz)) * (2 * sz)

                @pl.when((n & sz) != 0)
                def _():
                    fn(pltpu.make_async_copy(ybuf_ref.at[pl.ds(pl.multiple_of(c + done, RUN_ALIGN), sz)],
                                             xs_ref.at[pl.ds(pl.multiple_of(d + done, RUN_ALIGN), sz)], sem))

    each_piece(lambda cp: cp.start())
    each_piece(lambda cp: cp.wait())


def _dispatch(route, tile_np, tile_c, tile_d, meta, hn, tm, n_tiles):
    t = hn.shape[0]
    smem = pl.BlockSpec(memory_space=pltpu.SMEM)
    return pl.pallas_call(
        functools.partial(_dispatch_kernel, tm=tm, n_tiles=n_tiles),
        grid=(t // tm,),
        in_specs=[
            pl.BlockSpec((8, tm), lambda i: (0, i)),
            smem, smem, smem, smem,
            pl.BlockSpec((tm, D_MODEL), lambda i: (i, 0)),
        ],
        out_specs=pl.BlockSpec(memory_space=pl.ANY),
        out_shape=jax.ShapeDtypeStruct((n_tiles * MOE_TILE, D_MODEL), F32),
        scratch_shapes=[pltpu.VMEM((2 * tm + N_EXPERTS * RUN_ALIGN, D_MODEL), F32),
                        pltpu.VMEM((MOE_TILE, D_MODEL), F32),
                        pltpu.SemaphoreType.DMA, pltpu.SemaphoreType.DMA],
        compiler_params=_cparams(("arbitrary",)),
        name="moe_dispatch",
    )(route, tile_np, tile_c, tile_d, meta, hn)


def _gffn_kernel(te_ref, nv_ref, x3_ref, wg_ref, wu_ref, wd_ref, y3_ref, xb_ref, acc_ref):
    g = pl.program_id(0)
    f = pl.program_id(1)

    @pl.when((g >= nv_ref[0]) & (f == 0))
    def _():
        y3_ref[...] = jnp.zeros_like(y3_ref)

    @pl.when(g < nv_ref[0])
    def _():
        @pl.when(f == 0)
        def _():
            xb_ref[...] = x3_ref[...].astype(BF16)

        xb = xb_ref[...]
        h = (_silu(_dot(xb, wg_ref[0])) * _dot(xb, wu_ref[0])).astype(BF16)
        y = _dot(h, wd_ref[0])

        @pl.when(f == 0)
        def _():
            acc_ref[...] = y

        @pl.when(f == pl.num_programs(1) - 1)
        def _():
            y3_ref[...] = acc_ref[...] + y


def _grouped_ffn(tile_expert, n_valid, xs3, wg, wu, wd, tf):
    n_tiles = tile_expert.shape[0]
    assert D_FF // tf == 2
    grid_spec = pltpu.PrefetchScalarGridSpec(
        num_scalar_prefetch=2,
        grid=(n_tiles, D_FF // tf),
        in_specs=[
            pl.BlockSpec((MOE_TILE, D_MODEL), lambda g, f, te, nv: (g, 0)),
            pl.BlockSpec((1, D_MODEL, tf), lambda g, f, te, nv: (te[g], 0, f)),
            pl.BlockSpec((1, D_MODEL, tf), lambda g, f, te, nv: (te[g], 0, f)),
            pl.BlockSpec((1, tf, D_MODEL), lambda g, f, te, nv: (te[g], f, 0)),
        ],
        out_specs=pl.BlockSpec((MOE_TILE, D_MODEL), lambda g, f, te, nv: (g, 0)),
        scratch_shapes=[pltpu.VMEM((MOE_TILE, D_MODEL), BF16), pltpu.VMEM((MOE_TILE, D_MODEL), F32)],
    )
    return pl.pallas_call(
        _gffn_kernel,
        grid_spec=grid_spec,
        out_shape=jax.ShapeDtypeStruct(xs3.shape, F32),
        compiler_params=_cparams(("arbitrary", "arbitrary")),
        name="moe_ffn",
    )(tile_expert, n_valid, xs3, wg, wu, wd)


def _combine_kernel(routec_ref, w12_ref, x_ref, np_ref, c_ref, d_ref, y_ref, o_ref, ybuf_ref, sem, *, tm):
    i = pl.program_id(0)
    sizes = [1 << k for k in range(tm.bit_length() - 1, RUN_ALIGN.bit_length() - 2, -1)]

    def each_piece(fn):
        for e in range(N_EXPERTS):
            n = np_ref[i * N_EXPERTS + e]
            c = c_ref[i * N_EXPERTS + e]
            d = d_ref[i * N_EXPERTS + e]
            for sz in sizes:
                done = (n // (2 * sz)) * (2 * sz)

                @pl.when((n & sz) != 0)
                def _():
                    fn(pltpu.make_async_copy(y_ref.at[pl.ds(pl.multiple_of(d + done, RUN_ALIGN), sz)],
                                             ybuf_ref.at[pl.ds(pl.multiple_of(c + done, RUN_ALIGN), sz)], sem))

    @pl.when(i == 0)
    def _():
        ybuf_ref[...] = jnp.zeros_like(ybuf_ref)

    each_piece(lambda cp: cp.start())

    rc = routec_ref[...]
    e1, e2, p1, p2 = rc[:, 0:1], rc[:, 1:2], rc[:, 2:3], rc[:, 3:4]
    for e in range(N_EXPERTS):
        ce = c_ref[i * N_EXPERTS + e].astype(F32)
        p1 = p1 + jnp.where(e1 == float(e), ce, 0.0)
        p2 = p2 + jnp.where(e2 == float(e), ce, 0.0)
    w = w12_ref[...]
    kk = lax.broadcasted_iota(jnp.int32, (tm, ybuf_ref.shape[0]), 1).astype(F32)
    wm = jnp.where(kk == p1, w[:, 0:1], jnp.where(kk == p2, w[:, 1:2], 0.0))
    wm_hi = wm.astype(BF16)
    wm_lo = (wm - wm_hi.astype(F32)).astype(BF16)

    each_piece(lambda cp: cp.wait())
    yb = ybuf_ref[...].astype(BF16)
    o_ref[...] = x_ref[...] + _dot(wm_hi, yb) + _dot(wm_lo, yb)


def _combine(routec, w12, x2d, tile_np, tile_c, tile_d, y, tm):
    t = x2d.shape[0]
    smem = pl.BlockSpec(memory_space=pltpu.SMEM)
    return pl.pallas_call(
        functools.partial(_combine_kernel, tm=tm),
        grid=(t // tm,),
        in_specs=[
            pl.BlockSpec((tm, LANES), lambda i: (i, 0)),
            pl.BlockSpec((tm, LANES), lambda i: (i, 0)),
            pl.BlockSpec((tm, D_MODEL), lambda i: (i, 0)),
            smem, smem, smem,
            pl.BlockSpec(memory_space=pl.ANY),
        ],
        out_specs=pl.BlockSpec((tm, D_MODEL), lambda i: (i, 0)),
        out_shape=jax.ShapeDtypeStruct((t, D_MODEL), F32),
        scratch_shapes=[pltpu.VMEM((2 * tm + N_EXPERTS * RUN_ALIGN, D_MODEL), F32), pltpu.SemaphoreType.DMA],
        compiler_params=_cparams(("arbitrary",)),
        name="moe_combine",
    )(routec, w12, x2d, tile_np, tile_c, tile_d, y)


def _moe_tables(before_f, cnt_f, n_tiles):
    before = before_f[::8, :N_EXPERTS].astype(jnp.int32)
    total = cnt_f[:, :N_EXPERTS].astype(jnp.int32)
    n = jnp.concatenate([before[1:], total], axis=0) - before
    npad = (n + (RUN_ALIGN - 1)) // RUN_ALIGN * RUN_ALIGN
    c = jnp.cumsum(npad, axis=1) - npad
    rows = jnp.sum(npad, axis=0)
    per = (rows + (MOE_TILE - 1)) // MOE_TILE
    ends = jnp.cumsum(per)
    starts = ends - per
    n_valid = ends[-1:]
    g = jnp.minimum(jnp.arange(n_tiles, dtype=jnp.int32), n_valid - 1)
    tile_expert = jnp.sum((g[:, None] >= ends[None, :]).astype(jnp.int32), axis=1)
    last = jnp.where(per > 0, (ends - 1) * MOE_TILE, -1)
    meta = jnp.concatenate([starts * MOE_TILE, last, n_valid]).astype(jnp.int32)
    d = (starts * MOE_TILE)[None, :] + jnp.cumsum(npad, axis=0) - npad
    flat = lambda a: a.reshape(-1).astype(jnp.int32)
    return (tile_expert.astype(jnp.int32), n_valid.astype(jnp.int32), jnp.pad(meta, (0, LANES - meta.shape[0])),
            flat(npad), flat(c), flat(d))


def _ssd_consts():
    e_mat = np.zeros((LANES, SSD_INNER), np.float32)
    for h in range(SSD_HEADS):
        e_mat[h, h * SSD_HEAD_DIM:(h + 1) * SSD_HEAD_DIM] = 1.0
    tril = np.tril(np.ones((CHUNK, CHUNK), np.float32))
    dmask = np.tile(np.eye(CHUNK, dtype=np.float32), (1, SSD_HEADS))
    trilmask = np.tile(tril, (1, SSD_HEADS))
    blk = np.arange(256) // SSD_HEAD_DIM
    bdmask = (blk[:, None] == blk[None, :]).astype(np.float32)
    return (jnp.asarray(e_mat, BF16), jnp.asarray(tril, BF16), jnp.asarray(dmask, F32),
            jnp.asarray(trilmask, F32), jnp.asarray(bdmask, BF16))


def _rope_tables(s_len):
    half = ROPE_DIM // 2
    inv = ROPE_THETA ** (-jnp.arange(0, ROPE_DIM, 2, dtype=F32) / ROPE_DIM)
    ang = jnp.arange(s_len, dtype=F32)[:, None] * inv[None, :]
    cos, sin = jnp.cos(ang), jnp.sin(ang)
    pad = jnp.zeros((s_len, ATT_QK_DIM - ROPE_DIM), F32)
    cos_c = jnp.concatenate([cos, cos, pad + 1.0], axis=1)
    sa_c = jnp.concatenate([-sin, jnp.zeros_like(sin), pad], axis=1)
    sb_c = jnp.concatenate([jnp.zeros_like(sin), sin, pad], axis=1)
    two = lambda a: jnp.concatenate([a, a], axis=1)
    return two(cos_c), two(sa_c), two(sb_c)


def _split2_w(w):
    w = jnp.pad(w, ((0, 0), (0, LANES - w.shape[1])))
    hi = w.astype(BF16)
    lo = (w - hi.astype(F32)).astype(BF16)
    return jnp.stack([hi, lo])


def _pick(t, prefs):
    for p in prefs:
        if t % p == 0:
            return p
    return t


def kernel(x, norm_mix_w, w_in, conv_w, conv_b, dt_bias, a_log, d_skip, ssd_norm_w, q_norm_w, k_norm_w,
           lambda_q1, lambda_k1, lambda_q2, lambda_k2, subln_w, gate_b, w_br_ssd, w_br_att, w_out,
           norm_ffn_w, ffn_w_gate, ffn_w_up, ffn_w_down, router_w, moe_w_gate, moe_w_up, moe_w_down):
    bsz, s_len, d = x.shape
    depth = w_in.shape[0]
    t = bsz * s_len
    assert d == D_MODEL and s_len % 512 == 0

    tm_proj = _pick(t, (1024, 512))
    tm_row = _pick(t, (512,))
    lb = 256
    tm_qk = 512
    tq, heads_per_step = 512, 4

    consts = _ssd_consts()
    tril_tok = jnp.asarray(np.tril(np.ones((tm_row, tm_row), np.float32), -1), BF16)
    cos_t, sa_t, sb_t = _rope_tables(s_len)
    blk64 = np.arange(256) // ATT_QK_DIM
    gmat = jnp.asarray((blk64[:, None] == blk64[None, :]).astype(np.float32), BF16)
    pup = jnp.asarray(np.eye(256, k=-(ROPE_DIM // 2), dtype=np.float32), BF16)
    pdn = jnp.asarray(np.eye(256, k=ROPE_DIM // 2, dtype=np.float32), BF16)

    o1 = SSD_INNER
    o2 = o1 + SSD_CONV_DIM
    o3 = o2 + SSD_HEADS
    o4 = o3 + ATT_WIDTH
    o5 = o4 + ATT_WIDTH
    o6 = o5 + ATT_WIDTH
    o7 = o6 + D_MODEL

    x2d = x.reshape(t, d)
    for i in range(depth):
        lambda_init = 0.8 - 0.6 * math.exp(-0.3 * i)
        wi = w_in[i]
        w_main = jnp.concatenate([wi[:, :o1], wi[:, o3:], wi[:, o1:o2]], axis=1).astype(BF16)
        w_dt = _split2_w(wi[:, o2:o3])
        proj, dt_raw = _in_proj(x2d, norm_mix_w[i][None, :], w_main, w_dt, tm_proj, SSD_CONV_DIM)

        dtb = jnp.pad(dt_bias[i], (0, LANES - SSD_HEADS))[None, :]
        aexp = jnp.repeat(-jnp.exp(a_log[i].astype(F32)), SSD_HEAD_DIM)[None, :]
        dexp = jnp.repeat(d_skip[i], SSD_HEAD_DIM)[None, :]
        y_ssd = _ssd(proj, dt_raw, conv_w[i], conv_b[i][None, :], dtb, aexp, dexp, ssd_norm_w[i][None, :],
                     consts, bsz, s_len, lb)

        qw = jnp.tile(q_norm_w[i], ATT_WIDTH // ATT_QK_DIM)[None, :]
        kw = jnp.tile(k_norm_w[i], ATT_WIDTH // ATT_QK_DIM)[None, :]
        q_prep, k_t = _qk_prep(proj, qw, kw, cos_t, sa_t, sb_t, gmat, pup, pdn, bsz, s_len, tm_qk)
        y_att = _attention(q_prep, k_t, proj, lambda_q1[i][None, :], lambda_k1[i][None, :],
                           lambda_q2[i][None, :], lambda_k2[i][None, :], subln_w[i][None, :],
                           lambda_init, bsz, s_len, tq, heads_per_step)

        j = i // 2
        is_moe = i % 2 == 1
        rw = _split2_w(router_w[j]) if is_moe else None
        outs = _merge(y_ssd, y_att, proj, x2d, gate_b[i], w_br_ssd[i].astype(BF16), w_br_att[i].astype(BF16),
                      w_out[i].astype(BF16), norm_ffn_w[i][None, :], rw, tril_tok, tm_row)
        if is_moe:
            x2d, hn3, w12, route, routec, cnt_f, before_f = outs
            pad_rows = (t // tm_row) * N_EXPERTS * (RUN_ALIGN - 1)
            n_tiles = -(-(2 * t + pad_rows) // MOE_TILE) + N_EXPERTS
            tile_e, n_valid, meta, tile_np, tile_c, tile_d = _moe_tables(before_f, cnt_f, n_tiles)
            xs3 = _dispatch(route, tile_np, tile_c, tile_d, meta, hn3, tm_row, n_tiles)
            y3 = _grouped_ffn(tile_e, n_valid, xs3, moe_w_gate[j].astype(BF16),
                              moe_w_up[j].astype(BF16), moe_w_down[j].astype(BF16), D_FF // 2)
            x2d = _combine(routec, w12, x2d, tile_np, tile_c, tile_d, y3, tm_row)
        else:
            x2d, hn = outs
            x2d = _ffn(hn, x2d, ffn_w_gate[j].astype(BF16), ffn_w_up[j].astype(BF16),
                       ffn_w_down[j].astype(BF16), tm_row, D_FF // 2)
    return x2d.reshape(bsz, s_len, d)
```
